```python
import jax, jax.numpy as jnp
from jax import lax
import numpy as np

D_MODEL = 1024
BATCH = 8
SEQ = 4096
DEPTH = 4
DEC_BATCH = 8
DEC_SEQ = 8192
PAST_LEN = 128

PLE_DIM = 256
NORM_EPS = 1e-6
ATTN_HEADS = 16
ATTN_KV_HEADS = 4
HEAD_DIM = 64
ATTN_GROUP = ATTN_HEADS // ATTN_KV_HEADS
WINDOW = 128
ATTN_BLOCK = 128
ROPE_THETA = 500000.0
ROT_DIM = HEAD_DIM // 4
ATTN_Q = ATTN_HEADS * HEAD_DIM
ATTN_KV = ATTN_KV_HEADS * HEAD_DIM
SSM_INNER = D_MODEL
SSM_HEAD_DIM = 64
SSM_HEADS = SSM_INNER // SSM_HEAD_DIM
SSM_GROUPS = 2
SSM_STATE = 128
SSM_CONV = 5
SSM_CHUNK = 128
SSM_XBC = SSM_INNER + 2 * SSM_GROUPS * SSM_STATE
CONV_INNER = D_MODEL
CONV_WIDTH = 31
EVEN_SIZES = (ATTN_Q, ATTN_KV, ATTN_KV, ATTN_Q, SSM_XBC, SSM_INNER, 2 * SSM_HEADS)
EVEN_SPLITS = (ATTN_Q, ATTN_Q + ATTN_KV, ATTN_Q + 2 * ATTN_KV, 2 * ATTN_Q + 2 * ATTN_KV,
               2 * ATTN_Q + 2 * ATTN_KV + SSM_XBC, 2 * ATTN_Q + 2 * ATTN_KV + SSM_XBC + SSM_INNER)
EVEN_IN = 2 * ATTN_Q + 2 * ATTN_KV + SSM_XBC + SSM_INNER + 2 * SSM_HEADS
EVEN_MIX = ATTN_Q + SSM_INNER
ODD_IN = 3 * CONV_INNER

kernel_name = "hybrid_bidir_swa_ssd_conformer_trunk"


def rms_norm(x, g):
    xf = x.astype(jnp.float32)
    y = xf * lax.rsqrt(jnp.mean(xf * xf, axis=-1, keepdims=True) + NORM_EPS)
    return (y * g.astype(jnp.float32)).astype(x.dtype)


def layer_norm(x, g, b):
    xf = x.astype(jnp.float32)
    mu = jnp.mean(xf, axis=-1, keepdims=True)
    var = jnp.mean(jnp.square(xf - mu), axis=-1, keepdims=True)
    y = (xf - mu) * lax.rsqrt(var + NORM_EPS)
    return (y * g.astype(jnp.float32) + b.astype(jnp.float32)).astype(x.dtype)


def dwconv_centred(x, w, b):
    k = w.shape[0]
    out = lax.conv_general_dilated(x, w[:, None, :].astype(x.dtype), window_strides=(1,),
                                   padding=[(k // 2, k // 2)],
                                   dimension_numbers=("NWC", "WIO", "NWC"),
                                   feature_group_count=x.shape[-1])
    return out + b.astype(x.dtype)


def rope_tables(length):
    inv = 1.0 / (jnp.float32(ROPE_THETA) ** (jnp.arange(0, ROT_DIM, 2, dtype=jnp.float32) / ROT_DIM))
    ang = jnp.arange(length, dtype=jnp.float32)[:, None] * inv[None, :]
    return jnp.cos(ang), jnp.sin(ang)


def apply_partial_rope(t, cos, sin):
    half = ROT_DIM // 2
    c = cos[None, :, None, :].astype(t.dtype)
    s = sin[None, :, None, :].astype(t.dtype)
    r1, r2, rest = t[..., :half], t[..., half:ROT_DIM], t[..., ROT_DIM:]
    return jnp.concatenate([r1 * c - r2 * s, r2 * c + r1 * s, rest], axis=-1)


def band_mask(nb):
    qpos = jnp.arange(nb)[:, None, None] * ATTN_BLOCK + jnp.arange(ATTN_BLOCK)[None, :, None]
    kpos = (jnp.arange(nb)[:, None, None] - 1) * ATTN_BLOCK + jnp.arange(3 * ATTN_BLOCK)[None, None, :]
    return (jnp.abs(qpos - kpos) <= WINDOW) & (kpos >= 0) & (kpos < nb * ATTN_BLOCK)


def banded_sink_attention(q, k, v, sink):
    bsz, length = q.shape[0], q.shape[1]
    nb = length // ATTN_BLOCK
    qb = q.reshape(bsz, nb, ATTN_BLOCK, ATTN_KV_HEADS, ATTN_GROUP, HEAD_DIM)
    pad = ((0, 0), (ATTN_BLOCK, ATTN_BLOCK), (0, 0), (0, 0))
    kp = jnp.pad(k, pad).reshape(bsz, nb + 2, ATTN_BLOCK, ATTN_KV_HEADS, HEAD_DIM)
    vp = jnp.pad(v, pad).reshape(bsz, nb + 2, ATTN_BLOCK, ATTN_KV_HEADS, HEAD_DIM)
    kw = jnp.concatenate([kp[:, :-2], kp[:, 1:-1], kp[:, 2:]], axis=2)
    vw = jnp.concatenate([vp[:, :-2], vp[:, 1:-1], vp[:, 2:]], axis=2)
    s = jnp.einsum("bnqhgd,bnkhd->bnhgqk", qb, kw).astype(jnp.float32) * (HEAD_DIM ** -0.5)
    mask = band_mask(nb)[None, :, None, None]
    s = jnp.where(mask, s, jnp.float32(-1e30))
    sk = sink.astype(jnp.float32).reshape(1, 1, ATTN_KV_HEADS, ATTN_GROUP, 1, 1)
    m = jnp.maximum(jnp.max(s, axis=-1, keepdims=True), sk)
    e = jnp.exp(s - m)
    pr = e / (jnp.sum(e, axis=-1, keepdims=True) + jnp.exp(sk - m))
    o = jnp.einsum("bnhgqk,bnkhd->bnqhgd", pr.astype(v.dtype), vw)
    return o.reshape(bsz, length, ATTN_Q)


def ssd_chunked(x, dt, a, bm, cm):
    bsz, length, _, _ = x.shape
    nc = length // SSM_CHUNK
    r = SSM_HEADS // SSM_GROUPS
    x = x.reshape(bsz, nc, SSM_CHUNK, SSM_GROUPS, r, SSM_HEAD_DIM)
    dt = dt.reshape(bsz, nc, SSM_CHUNK, SSM_GROUPS, r)
    bm = bm.reshape(bsz, nc, SSM_CHUNK, SSM_GROUPS, SSM_STATE)
    cm = cm.reshape(bsz, nc, SSM_CHUNK, SSM_GROUPS, SSM_STATE)
    acs = jnp.cumsum(dt * a.reshape(SSM_GROUPS, r), axis=2)
    xdt = x * dt[..., None]
    seg = acs[:, :, :, None] - acs[:, :, None, :]
    tril = jnp.tril(jnp.ones((SSM_CHUNK, SSM_CHUNK), dtype=bool))[:, :, None, None]
    decay = jnp.exp(jnp.where(tril, seg, -jnp.inf))
    cb = jnp.einsum("bcign,bcjgn->bcijg", cm, bm)
    y_diag = jnp.einsum("bcijgr,bcjgrp->bcigrp", cb[..., None] * decay, xdt)
    to_end = jnp.exp(acs[:, :, -1:] - acs)
    states = jnp.einsum("bcjgn,bcjgr,bcjgrp->bcgrpn", bm, to_end, xdt)
    chunk_decay = jnp.exp(acs[:, :, -1])

    def step(s, inp):
        st, dec = inp
        return s * dec[..., None, None] + st, s

    init = jnp.zeros((bsz, SSM_GROUPS, r, SSM_HEAD_DIM, SSM_STATE), jnp.float32)
    _, s_prev = lax.scan(step, init, (jnp.moveaxis(states, 1, 0), jnp.moveaxis(chunk_decay, 1, 0)))
    s_prev = jnp.moveaxis(s_prev, 0, 1)
    y_off = jnp.einsum("bcign,bcgrpn->bcigrp", cm, s_prev) * jnp.exp(acs)[..., None]
    return (y_diag + y_off).reshape(bsz, length, SSM_HEADS, SSM_HEAD_DIM)


def even_mixer(u, w_in, w_out, sink, conv_w, conv_b, dt_bias, a_log, d_skip, ssm_norm_g, cos, sin):
    bsz, length, _ = u.shape
    proj = u @ w_in
    q, k, v, ga, xbc, z, dtr = jnp.split(proj, list(EVEN_SPLITS), axis=-1)
    q = apply_partial_rope(q.reshape(bsz, length, ATTN_HEADS, HEAD_DIM), cos, sin)
    k = apply_partial_rope(k.reshape(bsz, length, ATTN_KV_HEADS, HEAD_DIM), cos, sin)
    v = v.reshape(bsz, length, ATTN_KV_HEADS, HEAD_DIM)
    o_attn = banded_sink_attention(q, k, v, sink) * jax.nn.silu(ga)
    xbc = jax.nn.silu(dwconv_centred(xbc, conv_w, conv_b)).astype(jnp.float32)
    xs, bm, cm = jnp.split(xbc, [SSM_INNER, SSM_INNER + SSM_GROUPS * SSM_STATE], axis=-1)
    xs = xs.reshape(bsz, length, SSM_HEADS, SSM_HEAD_DIM)
    bm = bm.reshape(bsz, length, SSM_GROUPS, SSM_STATE)
    cm = cm.reshape(bsz, length, SSM_GROUPS, SSM_STATE)
    dt = jax.nn.softplus(dtr.astype(jnp.float32).reshape(bsz, length, 2, SSM_HEADS)
                         + dt_bias.astype(jnp.float32))
    a = -jnp.exp(a_log.astype(jnp.float32))
    fl = lambda t: jnp.flip(t, axis=1)
    y_f = ssd_chunked(xs, dt[:, :, 0], a[0], bm, cm)
    y_b = fl(ssd_chunked(fl(xs), fl(dt[:, :, 1]), a[1], fl(bm), fl(cm)))
    y = y_f + y_b + d_skip.astype(jnp.float32)[:, None] * xs
    y = y.reshape(bsz, length, SSM_INNER) * jax.nn.silu(z.astype(jnp.float32))
    o_ssm = rms_norm(y, ssm_norm_g).astype(u.dtype)
    return jnp.concatenate([o_attn, o_ssm], axis=-1) @ w_out


def odd_mixer(u, w_in, conv_w, conv_b, ln_g, ln_b, w_out):
    a, b, g = jnp.split(u @ w_in, 3, axis=-1)
    h = a * jax.nn.sigmoid(b)
    h = dwconv_centred(h, conv_w, conv_b)
    h = jax.nn.silu(layer_norm(h, ln_g, ln_b))
    return (h * jax.nn.silu(g)) @ w_out


def _trunk(x, p, norm_pre, norm_post, ple_w_gate, ple_w_proj, ple_norm, ev_w_in, ev_w_out,
           attn_sink, ssm_conv_w, ssm_conv_b, ssm_dt_bias, ssm_a_log, ssm_d, ssm_norm,
           od_w_in, od_conv_w, od_conv_b, od_ln_g, od_ln_b, od_w_out):
    cos, sin = rope_tables(x.shape[1])
    for i in range(DEPTH):
        j = i // 2
        u = rms_norm(x, norm_pre[i])
        if i % 2 == 0:
            y = even_mixer(u, ev_w_in[j], ev_w_out[j], attn_sink[j], ssm_conv_w[j], ssm_conv_b[j],
                           ssm_dt_bias[j], ssm_a_log[j], ssm_d[j], ssm_norm[j], cos, sin)
        else:
            y = odd_mixer(u, od_w_in[j], od_conv_w[j], od_conv_b[j], od_ln_g[j], od_ln_b[j], od_w_out[j])
        x = x + rms_norm(y, norm_post[i])
        gate = jax.nn.sigmoid(x @ ple_w_gate[i])
        x = x + rms_norm(gate * (p[i] @ ple_w_proj[i]), ple_norm[i])
    return x


def setup_inputs(seed: int = 0) -> dict:
    key = jax.random.key(seed)
    ks = jax.random.split(key, 28)
    ne, no = (DEPTH + 1) // 2, DEPTH // 2
    f32 = jnp.float32
    nrm = lambda k, shape, scale: jax.random.normal(k, shape, f32) * scale
    gain = lambda k, shape: 1.0 + 0.02 * jax.random.normal(k, shape, f32)
    dt0 = jnp.exp(jax.random.uniform(ks[14], (ne, 2, SSM_HEADS), f32, np.log(1e-3), np.log(1e-1)))
    return {
        "x_prompt": jax.random.normal(ks[0], (BATCH, SEQ, D_MODEL), f32),
        "x_sample": jax.random.normal(ks[1], (DEC_BATCH, DEC_SEQ, D_MODEL), f32),
        "p_prompt": jax.random.normal(ks[2], (DEPTH, BATCH, SEQ, PLE_DIM), f32),
        "p_sample": jax.random.normal(ks[3], (DEPTH, DEC_BATCH, DEC_SEQ, PLE_DIM), f32),
        "norm_pre": gain(ks[4], (DEPTH, D_MODEL)),
        "norm_post": gain(ks[5], (DEPTH, D_MODEL)),
        "ple_w_gate": nrm(ks[6], (DEPTH, D_MODEL, D_MODEL), D_MODEL ** -0.5),
        "ple_w_proj": nrm(ks[7], (DEPTH, PLE_DIM, D_MODEL), PLE_DIM ** -0.5),
        "ple_norm": gain(ks[8], (DEPTH, D_MODEL)),
        "ev_w_in": nrm(ks[9], (ne, D_MODEL, EVEN_IN), D_MODEL ** -0.5),
        "ev_w_out": nrm(ks[10], (ne, EVEN_MIX, D_MODEL), EVEN_MIX ** -0.5),
        "attn_sink": nrm(ks[11], (ne, ATTN_HEADS), 0.5),
        "ssm_conv_w": nrm(ks[12], (ne, SSM_CONV, SSM_XBC), SSM_CONV ** -0.5),
        "ssm_conv_b": nrm(ks[13], (ne, SSM_XBC), 0.02),
        "ssm_dt_bias": dt0 + jnp.log(-jnp.expm1(-dt0)),
        "ssm_a_log": jnp.log(jax.random.uniform(ks[15], (ne, 2, SSM_HEADS), f32, 1.0, 16.0)),
        "ssm_d": gain(ks[16], (ne, SSM_HEADS)),
        "ssm_norm": gain(ks[17], (ne, SSM_INNER)),
        "od_w_in": nrm(ks[18], (no, D_MODEL, ODD_IN), D_MODEL ** -0.5),
        "od_conv_w": nrm(ks[19], (no, CONV_WIDTH, CONV_INNER), CONV_WIDTH ** -0.5),
        "od_conv_b": nrm(ks[20], (no, CONV_INNER), 0.02),
        "od_ln_g": gain(ks[21], (no, CONV_INNER)),
        "od_ln_b": nrm(ks[22], (no, CONV_INNER), 0.02),
        "od_w_out": nrm(ks[23], (no, CONV_INNER, D_MODEL), CONV_INNER ** -0.5),
    }


def reference(x_prompt, x_sample, p_prompt, p_sample, norm_pre, norm_post, ple_w_gate, ple_w_proj,
              ple_norm, ev_w_in, ev_w_out, attn_sink, ssm_conv_w, ssm_conv_b, ssm_dt_bias, ssm_a_log,
              ssm_d, ssm_norm, od_w_in, od_conv_w, od_conv_b, od_ln_g, od_ln_b, od_w_out):
    weights = (norm_pre, norm_post, ple_w_gate, ple_w_proj, ple_norm, ev_w_in, ev_w_out, attn_sink,
               ssm_conv_w, ssm_conv_b, ssm_dt_bias, ssm_a_log, ssm_d, ssm_norm,
               od_w_in, od_conv_w, od_conv_b, od_ln_g, od_ln_b, od_w_out)
    y_prompt = _trunk(x_prompt, p_prompt, *weights)
    y_sample = _trunk(x_sample, p_sample, *weights)
    return (y_prompt, y_sample)
```

```python
import functools

import jax
import jax.numpy as jnp
from jax import lax
from jax.experimental import pallas as pl
from jax.experimental.pallas import tpu as pltpu

F32 = jnp.float32
BF16 = jnp.bfloat16

D_MODEL = 1024
DEPTH = 4
PLE_DIM = 256
NORM_EPS = 1e-6
ATTN_HEADS = 16
ATTN_KV_HEADS = 4
HEAD_DIM = 64
ATTN_GROUP = ATTN_HEADS // ATTN_KV_HEADS
ATTN_BLOCK = 128
ROPE_THETA = 500000.0
ROT_DIM = HEAD_DIM // 4
ATTN_Q = ATTN_HEADS * HEAD_DIM
ATTN_KV = ATTN_KV_HEADS * HEAD_DIM
SSM_INNER = D_MODEL
SSM_HEAD_DIM = 64
SSM_HEADS = SSM_INNER // SSM_HEAD_DIM
SSM_GROUPS = 2
SSM_STATE = 128
SSM_CONV = 5
SSM_CHUNK = 128
SSM_BC = 2 * SSM_GROUPS * SSM_STATE
SSM_XBC = SSM_INNER + SSM_BC
CONV_INNER = D_MODEL
CONV_WIDTH = 31

LANES = 128
SUBLANES = 8
HALO = 16
VMEM_LIMIT = 56 * 1024 * 1024

TOKEN_TILE = 512
ATTN_TILE = 512
CONV_TILE = 256
CONV_ROWS = 16


def _params(*sem):
    return pltpu.CompilerParams(dimension_semantics=sem, vmem_limit_bytes=VMEM_LIMIT)


def _const_spec(shape):
    nd = len(shape)
    return pl.BlockSpec(shape, lambda *_: (0,) * nd)


def _rms(x, g):
    ms = jnp.mean(x * x, axis=-1, keepdims=True)
    return x * lax.rsqrt(ms + NORM_EPS) * g


def _sigmoid(x):
    return jax.nn.sigmoid(x)


def _dot(a, b):
    return jnp.dot(a, b, preferred_element_type=F32)


def _even_in_kernel(x_ref, g_ref, wq_ref, wkv_ref, wga_ref, wxbc_ref, wz_ref, wdt_ref,
                    cos_ref, sa_ref, sb_ref,
                    q_ref, k_ref, v_ref, ga_ref, xbc_ref, z_ref, dt_ref):
    u = _rms(x_ref[...], g_ref[...]).astype(BF16)
    cos = cos_ref[...]
    sa = sa_ref[...]
    sb = sb_ref[...]

    def rope(t):
        return t * cos + pltpu.roll(t, 8, 1) * sa + pltpu.roll(t, LANES - 8, 1) * sb

    q = _dot(u, wq_ref[...])
    scale = HEAD_DIM ** -0.5
    for j in range(ATTN_Q // LANES):
        sl = slice(j * LANES, (j + 1) * LANES)
        q_ref[:, sl] = (rope(q[:, sl]) * scale).astype(BF16)
    kv = _dot(u, wkv_ref[...])
    for j in range(ATTN_KV // LANES):
        sl = slice(j * LANES, (j + 1) * LANES)
        k_ref[:, sl] = rope(kv[:, sl]).astype(BF16)
    v_ref[...] = kv[:, ATTN_KV:].astype(BF16)
    ga_ref[...] = _dot(u, wga_ref[...]).astype(BF16)
    xbc_ref[...] = _dot(u, wxbc_ref[...]).astype(BF16)
    z_ref[...] = _dot(u, wz_ref[...]).astype(BF16)
    dt_ref[...] = _dot(u, wdt_ref[...])


def _even_in(x2, g, wq, wkv, wga, wxbc, wz, wdt, cos, sa, sb, seq_len):
    t = x2.shape[0]
    tm = min(TOKEN_TILE, seq_len)
    per_seq = seq_len // tm
    row = lambda n: (n, 0)
    pos = lambda n: (n % per_seq, 0)
    outs = [(ATTN_Q, BF16), (ATTN_KV, BF16), (ATTN_KV, BF16), (ATTN_Q, BF16), (SSM_XBC, BF16),
            (SSM_INNER, BF16), (LANES, F32)]
    return pl.pallas_call(
        _even_in_kernel,
        grid=(t // tm,),
        in_specs=[pl.BlockSpec((tm, D_MODEL), row), _const_spec(g.shape),
                  _const_spec(wq.shape), _const_spec(wkv.shape), _const_spec(wga.shape),
                  _const_spec(wxbc.shape), _const_spec(wz.shape), _const_spec(wdt.shape),
                  pl.BlockSpec((tm, LANES), pos), pl.BlockSpec((tm, LANES), pos), pl.BlockSpec((tm, LANES), pos)],
        out_specs=[pl.BlockSpec((tm, w), row) for w, _ in outs],
        out_shape=[jax.ShapeDtypeStruct((t, w), d) for w, d in outs],
        compiler_params=_params("parallel"),
        name="even_in",
    )(x2, g, wq, wkv, wga, wxbc, wz, wdt, cos, sa, sb)


def _attn_kernel(sink_ref, q_ref, kc_ref, kp_ref, kn_ref, vc_ref, vp_ref, vn_ref, ga_ref, o_ref,
                 kw_ref, vw_ref, *, tq, nblk):
    n = pl.program_id(1)
    blk = ATTN_BLOCK
    kw_ref[0:blk] = kp_ref[0]
    kw_ref[blk:blk + tq] = kc_ref[0]
    kw_ref[blk + tq:] = kn_ref[0]
    vw_ref[0:blk] = vp_ref[0]
    vw_ref[blk:blk + tq] = vc_ref[0]
    vw_ref[blk + tq:] = vn_ref[0]
    r = lax.broadcasted_iota(jnp.int32, (blk, 3 * blk), 0)
    c = lax.broadcasted_iota(jnp.int32, (blk, 3 * blk), 1)
    band = (c >= r) & (c <= r + 2 * blk)

    def body(j, carry):
        qb = n * (tq // blk) + j
        valid = band & ((c >= blk) | (qb > 0)) & ((c < 2 * blk) | (qb < nblk - 1))
        row0 = pl.multiple_of(j * blk, blk)
        qblk = q_ref[0, pl.ds(row0, blk), :]
        kwin = kw_ref[pl.ds(row0, 3 * blk), :]
        vwin = vw_ref[pl.ds(row0, 3 * blk), :]
        outs = []
        for g in range(ATTN_KV_HEADS):
            kg = kwin[:, g * HEAD_DIM:(g + 1) * HEAD_DIM]
            vg = vwin[:, g * HEAD_DIM:(g + 1) * HEAD_DIM]
            for hh in range(ATTN_GROUP):
                h = g * ATTN_GROUP + hh
                qh = qblk[:, h * HEAD_DIM:(h + 1) * HEAD_DIM]
                s = lax.dot_general(qh, kg, (((1,), (1,)), ((), ())), preferred_element_type=F32)
                s = jnp.where(valid, s, F32(-1e30))
                sk = sink_ref[h]
                m = jnp.maximum(jnp.max(s, axis=-1, keepdims=True), sk)
                e = jnp.exp(s - m)
                den = jnp.sum(e, axis=-1, keepdims=True) + jnp.exp(sk - m)
                outs.append(_dot(e.astype(BF16), vg) / den)
        o_all = jnp.concatenate(outs, axis=1)
        ga = ga_ref[0, pl.ds(row0, blk), :].astype(F32)
        o_ref[0, pl.ds(row0, blk), :] = (o_all * (ga * _sigmoid(ga))).astype(BF16)
        return carry

    lax.fori_loop(0, tq // blk, body, 0)


def _attention(sink, q3, k3, v3, ga3):
    b, l, _ = q3.shape
    tq = min(ATTN_TILE, l)
    per = tq // ATTN_BLOCK
    nblk = l // ATTN_BLOCK
    cur = lambda bi, n: (bi, n, 0)
    prev = lambda bi, n: (bi, jnp.maximum(n * per - 1, 0), 0)
    nxt = lambda bi, n: (bi, jnp.minimum((n + 1) * per, nblk - 1), 0)
    kv_specs = [pl.BlockSpec((1, tq, ATTN_KV), cur), pl.BlockSpec((1, ATTN_BLOCK, ATTN_KV), prev),
                pl.BlockSpec((1, ATTN_BLOCK, ATTN_KV), nxt)]
    return pl.pallas_call(
        functools.partial(_attn_kernel, tq=tq, nblk=nblk),
        grid=(b, l // tq),
        in_specs=[pl.BlockSpec(memory_space=pltpu.SMEM), pl.BlockSpec((1, tq, ATTN_Q), cur)]
                 + kv_specs + kv_specs + [pl.BlockSpec((1, tq, ATTN_Q), cur)],
        out_specs=pl.BlockSpec((1, tq, ATTN_Q), cur),
        out_shape=jax.ShapeDtypeStruct((b, l, ATTN_Q), BF16),
        scratch_shapes=[pltpu.VMEM((tq + 2 * ATTN_BLOCK, ATTN_KV), BF16),
                        pltpu.VMEM((tq + 2 * ATTN_BLOCK, ATTN_KV), BF16)],
        compiler_params=_params("parallel", "parallel"),
        name="attention",
    )(sink, q3, k3, k3, k3, v3, v3, v3, ga3)


def _xbc_conv_kernel(xc_ref, xp_ref, xn_ref, w_ref, b_ref, o_ref, pad_ref, *, tc, ntile):
    n = pl.program_id(1)
    prev = xp_ref[0].astype(F32)
    nxt = xn_ref[0].astype(F32)
    pad_ref[0:HALO] = jnp.where(n > 0, prev, 0.0)
    pad_ref[HALO:HALO + tc] = xc_ref[0].astype(F32)
    pad_ref[HALO + tc:] = jnp.where(n < ntile - 1, nxt, 0.0)
    half = SSM_CONV // 2
    acc = jnp.zeros((tc, SSM_XBC), F32) + b_ref[...]
    for k in range(SSM_CONV):
        acc = acc + pad_ref[pl.ds(HALO - half + k, tc), :] * w_ref[k:k + 1, :]
    o_ref[0] = (acc * _sigmoid(acc)).astype(BF16)


def _xbc_conv(xbc3, w, b):
    bsz, l, _ = xbc3.shape
    tc = min(TOKEN_TILE, l)
    ntile = l // tc
    per = tc // HALO
    nh = l // HALO
    cur = lambda bi, n: (bi, n, 0)
    prev = lambda bi, n: (bi, jnp.maximum(n * per - 1, 0), 0)
    nxt = lambda bi, n: (bi, jnp.minimum((n + 1) * per, nh - 1), 0)
    return pl.pallas_call(
        functools.partial(_xbc_conv_kernel, tc=tc, ntile=ntile),
        grid=(bsz, ntile),
        in_specs=[pl.BlockSpec((1, tc, SSM_XBC), cur), pl.BlockSpec((1, HALO, SSM_XBC), prev),
                  pl.BlockSpec((1, HALO, SSM_XBC), nxt), _const_spec(w.shape), _const_spec(b.shape)],
        out_specs=pl.BlockSpec((1, tc, SSM_XBC), cur),
        out_shape=jax.ShapeDtypeStruct((bsz, l, SSM_XBC), BF16),
        scratch_shapes=[pltpu.VMEM((tc + 2 * HALO, SSM_XBC), F32)],
        compiler_params=_params("parallel", "parallel"),
        name="xbc_conv",
    )(xbc3, xbc3, xbc3, w, b)


def _pair_cols(arr, h0, lo_mask):
    return jnp.where(lo_mask, arr[:, h0:h0 + 1], arr[:, h0 + 1:h0 + 2])


def _ssd_kernel(*refs, rev):
    if rev:
        (xbc_ref, dt_ref, bias_ref, alog_ref, yf_ref, z_ref, dskip_ref, gn_ref, out_ref, st_ref) = refs
    else:
        (xbc_ref, dt_ref, bias_ref, alog_ref, out_ref, st_ref) = refs
    ch = SSM_CHUNK
    hp = SSM_HEADS // SSM_GROUPS
    gw = hp * SSM_HEAD_DIM
    off = SSM_HEADS if rev else 0

    @pl.when(pl.program_id(1) == 0)
    def _():
        st_ref[...] = jnp.zeros_like(st_ref)

    dt = jax.nn.softplus(dt_ref[0] + bias_ref[...])
    a = -jnp.exp(alog_ref[...])
    v = dt * a
    ri = lax.broadcasted_iota(jnp.int32, (ch, ch), 0)
    ci = lax.broadcasted_iota(jnp.int32, (ch, ch), 1)
    tri = (ci >= ri) if rev else (ci <= ri)
    trib = jnp.where(tri, 1.0, 0.0).astype(BF16)
    v1 = v.astype(BF16)
    r1 = v - v1.astype(F32)
    v2 = r1.astype(BF16)
    v3 = (r1 - v2.astype(F32)).astype(BF16)
    acs = _dot(trib, v1) + _dot(trib, v2) + _dot(trib, v3)
    tot = acs[0:1, :] if rev else acs[ch - 1:ch, :]
    eacs = jnp.exp(acs)
    te = jnp.exp(tot - acs) * dt
    cdec = jnp.exp(tot)
    acs_t = acs.T
    dt_t = dt.T
    lane = lax.broadcasted_iota(jnp.int32, (ch, LANES), 1)
    lo = lane < SSM_HEAD_DIM
    lo1 = lo[0:1, :]

    xs = xbc_ref[0, :, 0:SSM_INNER]
    ys = []
    for g in range(SSM_GROUPS):
        bg = xbc_ref[0, :, SSM_INNER + g * SSM_STATE:SSM_INNER + (g + 1) * SSM_STATE]
        cg = xbc_ref[0, :, SSM_INNER + (SSM_GROUPS + g) * SSM_STATE:SSM_INNER + (SSM_GROUPS + g + 1) * SSM_STATE]
        cb = lax.dot_general(cg, bg, (((1,), (1,)), ((), ())), preferred_element_type=F32)
        st_prev = st_ref[g]
        te_x, eacs_x, cdec_x, yd = [], [], [], []
        for k in range(hp // 2):
            h0 = off + g * hp + 2 * k
            xp = xs[:, g * gw + k * LANES:g * gw + (k + 1) * LANES]
            ms = []
            for e in range(2):
                h = h0 + e
                seg = acs[:, h:h + 1] - acs_t[h:h + 1, :]
                dec = jnp.exp(jnp.where(tri, seg, -jnp.inf))
                ms.append((cb * dec * dt_t[h:h + 1, :]).astype(BF16))
            lhs = jnp.concatenate(ms, axis=1)
            zero = jnp.zeros_like(xp)
            rhs = jnp.concatenate([jnp.where(lo, xp, zero), jnp.where(lo, zero, xp)], axis=0)
            yd.append(_dot(lhs, rhs))
            te_x.append(_pair_cols(te, h0, lo))
            eacs_x.append(_pair_cols(eacs, h0, lo))
            cdec_x.append(_pair_cols(cdec, h0, lo1))
        xg = xs[:, g * gw:(g + 1) * gw].astype(F32)
        xend = (xg * jnp.concatenate(te_x, axis=1)).astype(BF16)
        st_new = lax.dot_general(bg, xend, (((0,), (0,)), ((), ())), preferred_element_type=F32)
        y_off = _dot(cg, st_prev.astype(BF16)) * jnp.concatenate(eacs_x, axis=1)
        st_ref[g] = st_prev * jnp.concatenate(cdec_x, axis=1) + st_new
        ys.append(jnp.concatenate(yd, axis=1) + y_off)
    y = jnp.concatenate(ys, axis=1)
    if rev:
        y = y + yf_ref[0] + dskip_ref[...] * xs.astype(F32)
        z = z_ref[0].astype(F32)
        y = y * (z * _sigmoid(z))
        out_ref[0] = _rms(y, gn_ref[...]).astype(BF16)
    else:
        out_ref[0] = y


def _ssd(xbcc3, dt3, bias, alog, rev, extra=()):
    b, l, _ = xbcc3.shape
    nc = l // SSM_CHUNK
    idx = (lambda bi, c: (bi, nc - 1 - c, 0)) if rev else (lambda bi, c: (bi, c, 0))
    blk = lambda w: pl.BlockSpec((1, SSM_CHUNK, w), idx)
    in_specs = [blk(SSM_XBC), blk(LANES), _const_spec(bias.shape), _const_spec(alog.shape)]
    if rev:
        yf, z3, dskip, gn = extra
        in_specs += [blk(SSM_INNER), blk(SSM_INNER), _const_spec(dskip.shape), _const_spec(gn.shape)]
    return pl.pallas_call(
        functools.partial(_ssd_kernel, rev=rev),
        grid=(b, nc),
        in_specs=in_specs,
        out_specs=blk(SSM_INNER),
        out_shape=jax.ShapeDtypeStruct((b, l, SSM_INNER), BF16 if rev else F32),
        scratch_shapes=[pltpu.VMEM((SSM_GROUPS, SSM_STATE, SSM_INNER // SSM_GROUPS), F32)],
        compiler_params=_params("parallel", "arbitrary"),
        name="ssd_bwd" if rev else "ssd_fwd",
    )(xbcc3, dt3, bias, alog, *extra)


def _tail(y, x, p, gpost, wg, wp, gple):
    x1 = x + _rms(y, gpost)
    gate = _sigmoid(_dot(x1.astype(BF16), wg))
    pp = _dot(p.astype(BF16), wp)
    return x1 + _rms(gate * pp, gple)


def _even_out_kernel(oa_ref, ob_ref, x_ref, p_ref, woa_ref, wob_ref, gpost_ref, wg_ref, wp_ref, gple_ref,
                     out_ref):
    y = _dot(oa_ref[...], woa_ref[...]) + _dot(ob_ref[...], wob_ref[...])
    out_ref[...] = _tail(y, x_ref[...], p_ref[...], gpost_ref[...], wg_ref[...], wp_ref[...], gple_ref[...])


def _even_out(oa, ob, x2, p2, woa, wob, gpost, wg, wp, gple, seq_len):
    t = x2.shape[0]
    tm = min(TOKEN_TILE, seq_len)
    row = lambda n: (n, 0)
    consts = (woa, wob, gpost, wg, wp, gple)
    return pl.pallas_call(
        _even_out_kernel,
        grid=(t // tm,),
        in_specs=[pl.BlockSpec((tm, ATTN_Q), row), pl.BlockSpec((tm, SSM_INNER), row),
                  pl.BlockSpec((tm, D_MODEL), row), pl.BlockSpec((tm, PLE_DIM), row)]
                 + [_const_spec(c.shape) for c in consts],
        out_specs=pl.BlockSpec((tm, D_MODEL), row),
        out_shape=jax.ShapeDtypeStruct((t, D_MODEL), F32),
        compiler_params=_params("parallel"),
        name="even_out",
    )(oa, ob, x2, p2, *consts)


def _odd_in_kernel(x_ref, g_ref, wa_ref, wb_ref, wg_ref, h_ref, sg_ref):
    u = _rms(x_ref[...], g_ref[...]).astype(BF16)
    a = _dot(u, wa_ref[...])
    b = _dot(u, wb_ref[...])
    h_ref[...] = (a * _sigmoid(b)).astype(BF16)
    g = _dot(u, wg_ref[...])
    sg_ref[...] = (g * _sigmoid(g)).astype(BF16)


def _odd_in(x2, g, wa, wb, wg, seq_len):
    t = x2.shape[0]
    tm = min(TOKEN_TILE, seq_len)
    row = lambda n: (n, 0)
    return pl.pallas_call(
        _odd_in_kernel,
        grid=(t // tm,),
        in_specs=[pl.BlockSpec((tm, D_MODEL), row), _const_spec(g.shape), _const_spec(wa.shape),
                  _const_spec(wb.shape), _const_spec(wg.shape)],
        out_specs=[pl.BlockSpec((tm, CONV_INNER), row), pl.BlockSpec((tm, CONV_INNER), row)],
        out_shape=[jax.ShapeDtypeStruct((t, CONV_INNER), BF16), jax.ShapeDtypeStruct((t, CONV_INNER), BF16)],
        compiler_params=_params("parallel"),
        name="odd_in",
    )(x2, g, wa, wb, wg)


def _odd_out_kernel(hc_ref, hp_ref, hn_ref, sg_ref, x_ref, p_ref, cw_ref, cb_ref, lng_ref, lnb_ref, wo_ref,
                    gpost_ref, wg_ref, wp_ref, gple_ref, out_ref, pad_ref, sh_ref, hm_ref, *, tc, ntile):
    n = pl.program_id(1)
    pad_ref[0:HALO] = jnp.where(n > 0, hp_ref[0].astype(F32), 0.0)
    pad_ref[HALO:HALO + tc] = hc_ref[0].astype(F32)
    pad_ref[HALO + tc:] = jnp.where(n < ntile - 1, hn_ref[0].astype(F32), 0.0)
    half = CONV_WIDTH // 2
    rows = CONV_ROWS
    span = sh_ref.shape[1]
    for c in range(SUBLANES):
        sh_ref[c] = pad_ref[pl.ds(c, span), :]

    def body(i, carry):
        r0 = pl.multiple_of(i * rows, rows)
        acc = jnp.zeros((rows, CONV_INNER), F32) + cb_ref[...]
        for k in range(CONV_WIDTH):
            o = HALO - half + k
            acc = acc + sh_ref[o % SUBLANES, pl.ds(r0 + o - o % SUBLANES, rows), :] * cw_ref[k:k + 1, :]
        mu = jnp.mean(acc, axis=-1, keepdims=True)
        cen = acc - mu
        var = jnp.mean(cen * cen, axis=-1, keepdims=True)
        hn = cen * lax.rsqrt(var + NORM_EPS) * lng_ref[...] + lnb_ref[...]
        hn = hn * _sigmoid(hn)
        hm_ref[pl.ds(r0, rows), :] = (hn * sg_ref[0, pl.ds(r0, rows), :].astype(F32)).astype(BF16)
        return carry

    lax.fori_loop(0, tc // rows, body, 0)
    y = _dot(hm_ref[...], wo_ref[...])
    out_ref[0] = _tail(y, x_ref[0], p_ref[0], gpost_ref[...], wg_ref[...], wp_ref[...], gple_ref[...])


def _odd_out(h3, sg3, x3, p3, cw, cb, lng, lnb, wo, gpost, wg, wp, gple):
    b, l, _ = h3.shape
    tc = min(CONV_TILE, l)
    ntile = l // tc
    per = tc // HALO
    nh = l // HALO
    cur = lambda bi, n: (bi, n, 0)
    prev = lambda bi, n: (bi, jnp.maximum(n * per - 1, 0), 0)
    nxt = lambda bi, n: (bi, jnp.minimum((n + 1) * per, nh - 1), 0)
    consts = (cw, cb, lng, lnb, wo, gpost, wg, wp, gple)
    return pl.pallas_call(
        functools.partial(_odd_out_kernel, tc=tc, ntile=ntile),
        grid=(b, ntile),
        in_specs=[pl.BlockSpec((1, tc, CONV_INNER), cur), pl.BlockSpec((1, HALO, CONV_INNER), prev),
                  pl.BlockSpec((1, HALO, CONV_INNER), nxt), pl.BlockSpec((1, tc, CONV_INNER), cur),
                  pl.BlockSpec((1, tc, D_MODEL), cur), pl.BlockSpec((1, tc, PLE_DIM), cur)]
                 + [_const_spec(c.shape) for c in consts],
        out_specs=pl.BlockSpec((1, tc, D_MODEL), cur),
        out_shape=jax.ShapeDtypeStruct((b, l, D_MODEL), F32),
        scratch_shapes=[pltpu.VMEM((tc + 2 * HALO, CONV_INNER), F32),
                        pltpu.VMEM((SUBLANES, tc + 2 * HALO - SUBLANES, CONV_INNER), F32),
                        pltpu.VMEM((tc, CONV_INNER), BF16)],
        compiler_params=_params("parallel", "parallel"),
        name="odd_out",
    )(h3, h3, h3, sg3, x3, p3, *consts)


def _rope_lane_tables(length):
    inv = 1.0 / (jnp.float32(ROPE_THETA) ** (jnp.arange(0, ROT_DIM, 2, dtype=F32) / ROT_DIM))
    ang = jnp.arange(length, dtype=F32)[:, None] * inv[None, :]
    cos, sin = jnp.cos(ang), jnp.sin(ang)
    m = jnp.arange(LANES) % HEAD_DIM
    idx = m % (ROT_DIM // 2)
    cos_l = jnp.where(m < ROT_DIM, cos[:, idx], 1.0)
    sin_l = sin[:, idx]
    sa = jnp.where((m >= ROT_DIM // 2) & (m < ROT_DIM), sin_l, 0.0)
    sb = jnp.where(m < ROT_DIM // 2, -sin_l, 0.0)
    return cos_l, sa, sb


def _row(v):
    return v.reshape(1, -1).astype(F32)


def _pad_rows(w, rows):
    return jnp.concatenate([w, jnp.zeros((rows - w.shape[0],) + w.shape[1:], w.dtype)], axis=0)


def _pad_lanes(v, lanes=LANES):
    v = v.reshape(1, -1).astype(F32)
    return jnp.concatenate([v, jnp.zeros((1, lanes - v.shape[1]), F32)], axis=1)


def _even_weights(j, ev_w_in, ev_w_out, attn_sink, ssm_conv_w, ssm_conv_b, ssm_dt_bias, ssm_a_log, ssm_d,
                  ssm_norm):
    w = ev_w_in[j].astype(BF16)
    c0 = 0
    cols = []
    for width in (ATTN_Q, 2 * ATTN_KV, ATTN_Q, SSM_XBC, SSM_INNER, 2 * SSM_HEADS):
        cols.append(w[:, c0:c0 + width])
        c0 += width
    wq, wkv, wga, wxbc, wz, wdt = cols
    wdt = jnp.concatenate([wdt, jnp.zeros((D_MODEL, LANES - 2 * SSM_HEADS), BF16)], axis=1)
    wo = ev_w_out[j].astype(BF16)
    return dict(
        wq=wq, wkv=wkv, wga=wga, wxbc=wxbc, wz=wz, wdt=wdt, woa=wo[:ATTN_Q], wob=wo[ATTN_Q:],
        sink=attn_sink[j].astype(F32),
        conv_w=_pad_rows(ssm_conv_w[j].astype(F32), 8), conv_b=_row(ssm_conv_b[j]),
        dt_bias=_pad_lanes(ssm_dt_bias[j]), a_log=_pad_lanes(ssm_a_log[j]),
        dskip=_row(jnp.repeat(ssm_d[j].astype(F32), SSM_HEAD_DIM)), ssm_norm=_row(ssm_norm[j]))


def _odd_weights(j, od_w_in, od_conv_w, od_conv_b, od_ln_g, od_ln_b, od_w_out):
    w = od_w_in[j].astype(BF16)
    return dict(
        wa=w[:, :CONV_INNER], wb=w[:, CONV_INNER:2 * CONV_INNER], wg=w[:, 2 * CONV_INNER:],
        conv_w=_pad_rows(od_conv_w[j].astype(F32), 32), conv_b=_row(od_conv_b[j]),
        ln_g=_row(od_ln_g[j]), ln_b=_row(od_ln_b[j]), wo=od_w_out[j].astype(BF16))


def _trunk(x, p, layers, rope):
    b, l, _ = x.shape
    t = b * l
    flat = lambda a: a.reshape(t, a.shape[-1])
    seq = lambda a: a.reshape(b, l, a.shape[-1])
    cos, sa, sb = rope
    for i, lw in enumerate(layers):
        common = (lw["gpost"], lw["ple_wg"], lw["ple_wp"], lw["gple"])
        if i % 2 == 0:
            q, k, v, ga, xbc, z, dt = _even_in(flat(x), lw["gpre"], lw["wq"], lw["wkv"], lw["wga"], lw["wxbc"],
                                               lw["wz"], lw["wdt"], cos, sa, sb, l)
            o_attn = _attention(lw["sink"], seq(q), seq(k), seq(v), seq(ga))
            xbcc = _xbc_conv(seq(xbc), lw["conv_w"], lw["conv_b"])
            dt3 = seq(dt)
            y_f = _ssd(xbcc, dt3, lw["dt_bias"], lw["a_log"], rev=False)
            o_ssm = _ssd(xbcc, dt3, lw["dt_bias"], lw["a_log"], rev=True,
                         extra=(y_f, seq(z), lw["dskip"], lw["ssm_norm"]))
            x = seq(_even_out(flat(o_attn), flat(o_ssm), flat(x), flat(p[i]), lw["woa"], lw["wob"], *common, l))
        else:
            h, sg = _odd_in(flat(x), lw["gpre"], lw["wa"], lw["wb"], lw["wg"], l)
            x = _odd_out(seq(h), seq(sg), x, p[i], lw["conv_w"], lw["conv_b"], lw["ln_g"], lw["ln_b"], lw["wo"],
                         *common)
    return x


def kernel(x_prompt, x_sample, p_prompt, p_sample, norm_pre, norm_post, ple_w_gate, ple_w_proj, ple_norm, ev_w_in, ev_w_out, attn_sink, ssm_conv_w, ssm_conv_b, ssm_dt_bias, ssm_a_log, ssm_d, ssm_norm, od_w_in, od_conv_w, od_conv_b, od_ln_g, od_ln_b, od_w_out):
    layers = []
    for i in range(DEPTH):
        j = i // 2
        if i % 2 == 0:
            lw = _even_weights(j, ev_w_in, ev_w_out, attn_sink, ssm_conv_w, ssm_conv_b, ssm_dt_bias, ssm_a_log,
                               ssm_d, ssm_norm)
        else:
            lw = _odd_weights(j, od_w_in, od_conv_w, od_conv_b, od_ln_g, od_ln_b, od_w_out)
        lw.update(gpre=_row(norm_pre[i]), gpost=_row(norm_post[i]), ple_wg=ple_w_gate[i].astype(BF16),
                  ple_wp=ple_w_proj[i].astype(BF16), gple=_row(ple_norm[i]))
        layers.append(lw)
    y_prompt = _trunk(x_prompt, p_prompt, layers, _rope_lane_tables(x_prompt.shape[1]))
    y_sample = _trunk(x_sample, p_sample, layers, _rope_lane_tables(x_sample.shape[1]))
    return (y_prompt, y_sample)
```

```python
import functools

import jax
import jax.numpy as jnp
from jax import lax
from jax.experimental import pallas as pl
from jax.experimental.pallas import tpu as pltpu

F32 = jnp.float32
BF16 = jnp.bfloat16

D_MODEL = 1024
DEPTH = 4
PLE_DIM = 256
NORM_EPS = 1e-6
ATTN_HEADS = 16
ATTN_KV_HEADS = 4
HEAD_DIM = 64
ATTN_GROUP = ATTN_HEADS // ATTN_KV_HEADS
ATTN_BLOCK = 128
ROPE_THETA = 500000.0
ROT_DIM = HEAD_DIM // 4
ROT_HALF = ROT_DIM // 2
ATTN_Q = ATTN_HEADS * HEAD_DIM
ATTN_KV = ATTN_KV_HEADS * HEAD_DIM
SSM_INNER = D_MODEL
SSM_HEAD_DIM = 64
SSM_HEADS = SSM_INNER // SSM_HEAD_DIM
SSM_GROUPS = 2
SSM_STATE = 128
SSM_CONV = 5
SSM_CHUNK = 128
SSM_BC = 2 * SSM_GROUPS * SSM_STATE
SSM_XBC = SSM_INNER + SSM_BC
CONV_INNER = D_MODEL
CONV_WIDTH = 31

LANES = 128
SUBLANES = 8
HALO = 16
VMEM_LIMIT = 56 * 1024 * 1024

TOKEN_TILE = 512
ATTN_TILE = 512
CONV_TILE = 256
CONV_ROWS = 32
CONV_LANES = 512

NT_DIMS = (((1,), (1,)), ((), ()))
LOG2E = 1.4426950408889634


def _params(*sem):
    return pltpu.CompilerParams(dimension_semantics=sem, vmem_limit_bytes=VMEM_LIMIT)


def _const_spec(shape):
    nd = len(shape)
    return pl.BlockSpec(shape, lambda *_: (0,) * nd, pipeline_mode=pl.Buffered(1))


def _rms(x, g):
    ms = jnp.mean(x * x, axis=-1, keepdims=True)
    return x * lax.rsqrt(ms + NORM_EPS) * g


def _sigmoid(x):
    return jax.nn.sigmoid(x)


def _silu(x):
    return x * _sigmoid(x)


def _dot(a, b):
    return jnp.dot(a, b, preferred_element_type=F32)


def _dot_nt(a, b):
    return lax.dot_general(a, b, NT_DIMS, preferred_element_type=F32)


def _halo_maps(per, nh):
    prev = lambda bi, n: (bi, jnp.maximum(n * per - 1, 0), 0)
    nxt = lambda bi, n: (bi, jnp.minimum((n + 1) * per, nh - 1), 0)
    return prev, nxt


def _even_in_kernel(xc_ref, xp_ref, xn_ref, g_ref, wqt_ref, wk_ref, wvt_ref, wga_ref, wxbc_ref, wz_ref, wdt_ref,
                    cos_ref, sa_ref, sb_ref, cost_ref, sint_ref, cw_ref, cb_ref,
                    qt_ref, k_ref, vt_ref, ga_ref, xbc_ref, z_ref, dt_ref, pad_ref, *, tm, ntile):
    n = pl.program_id(1)
    g = g_ref[...]
    u = _rms(xc_ref[0], g).astype(BF16)

    qt = _dot_nt(wqt_ref[...], u)
    ct = cost_ref[...]
    st = sint_ref[...]
    parts = []
    for h in range(ATTN_HEADS):
        b0 = h * HEAD_DIM
        r1 = qt[b0:b0 + ROT_HALF]
        r2 = qt[b0 + ROT_HALF:b0 + ROT_DIM]
        parts += [r1 * ct - r2 * st, r2 * ct + r1 * st, qt[b0 + ROT_DIM:b0 + HEAD_DIM]]
    scale = HEAD_DIM ** -0.5 * LOG2E
    qt = (jnp.concatenate(parts, axis=0) * scale).astype(BF16)
    vt = _dot_nt(wvt_ref[...], u).astype(BF16)
    for j in range(tm // ATTN_BLOCK):
        sl = slice(j * ATTN_BLOCK, (j + 1) * ATTN_BLOCK)
        qt_ref[0, j] = qt[:, sl]
        vt_ref[0, j] = vt[:, sl]

    cos = cos_ref[...]
    sa = sa_ref[...]
    sb = sb_ref[...]
    k = _dot(u, wk_ref[...])
    for j in range(ATTN_KV // LANES):
        sl = slice(j * LANES, (j + 1) * LANES)
        t = k[:, sl]
        k_ref[0, :, sl] = (t * cos + pltpu.roll(t, ROT_HALF, 1) * sa
                           + pltpu.roll(t, LANES - ROT_HALF, 1) * sb).astype(BF16)

    ga_ref[0] = _dot(u, wga_ref[...]).astype(BF16)
    z_ref[0] = _dot(u, wz_ref[...]).astype(BF16)
    dt_ref[0] = _dot(u, wdt_ref[...])

    xh = jnp.concatenate([xp_ref[0], xn_ref[0]], axis=0)
    uh = _rms(xh, g).astype(BF16)
    hal = _dot(uh, wxbc_ref[...])
    pad_ref[0:SUBLANES] = jnp.where(n > 0, hal[0:SUBLANES], 0.0)
    pad_ref[SUBLANES:SUBLANES + tm] = _dot(u, wxbc_ref[...])
    pad_ref[SUBLANES + tm:] = jnp.where(n < ntile - 1, hal[SUBLANES:], 0.0)
    half = SSM_CONV // 2
    acc = jnp.zeros((tm, SSM_XBC), F32) + cb_ref[...]
    for kk in range(SSM_CONV):
        acc = acc + pad_ref[pl.ds(SUBLANES - half + kk, tm), :] * cw_ref[kk:kk + 1, :]
    xbc_ref[0] = _silu(acc).astype(BF16)


def _even_in(x3, g, lw, rope):
    b, l, _ = x3.shape
    tm = min(TOKEN_TILE, l)
    ntile = l // tm
    nb = tm // ATTN_BLOCK
    cos, sa, sb, cos_t, sin_t = rope
    cur = lambda bi, n: (bi, n, 0)
    prev, nxt = _halo_maps(tm // SUBLANES, l // SUBLANES)
    cur4 = lambda bi, n: (bi, n, 0, 0)
    pos = lambda bi, n: (n, 0)
    pos_t = lambda bi, n: (0, n)
    consts = (g, lw["wqt"], lw["wk"], lw["wvt"], lw["wga"], lw["wxbc"], lw["wz"], lw["wdt"])
    tail = (lw["conv_w"], lw["conv_b"])
    out_shape = [jax.ShapeDtypeStruct((b, l // ATTN_BLOCK, ATTN_Q, ATTN_BLOCK), BF16),
                 jax.ShapeDtypeStruct((b, l, ATTN_KV), BF16),
                 jax.ShapeDtypeStruct((b, l // ATTN_BLOCK, ATTN_KV, ATTN_BLOCK), BF16),
                 jax.ShapeDtypeStruct((b, l, ATTN_Q), BF16),
                 jax.ShapeDtypeStruct((b, l, SSM_XBC), BF16),
                 jax.ShapeDtypeStruct((b, l, SSM_INNER), BF16),
                 jax.ShapeDtypeStruct((b, l, LANES), F32)]
    out_specs = [pl.BlockSpec((1, nb, ATTN_Q, ATTN_BLOCK), cur4), pl.BlockSpec((1, tm, ATTN_KV), cur),
                 pl.BlockSpec((1, nb, ATTN_KV, ATTN_BLOCK), cur4), pl.BlockSpec((1, tm, ATTN_Q), cur),
                 pl.BlockSpec((1, tm, SSM_XBC), cur), pl.BlockSpec((1, tm, SSM_INNER), cur),
                 pl.BlockSpec((1, tm, LANES), cur)]
    return pl.pallas_call(
        functools.partial(_even_in_kernel, tm=tm, ntile=ntile),
        grid=(b, ntile),
        in_specs=[pl.BlockSpec((1, tm, D_MODEL), cur), pl.BlockSpec((1, SUBLANES, D_MODEL), prev),
                  pl.BlockSpec((1, SUBLANES, D_MODEL), nxt)]
                 + [_const_spec(c.shape) for c in consts]
                 + [pl.BlockSpec((tm, LANES), pos)] * 3 + [pl.BlockSpec((ROT_HALF, tm), pos_t)] * 2
                 + [_const_spec(c.shape) for c in tail],
        out_specs=out_specs,
        out_shape=out_shape,
        scratch_shapes=[pltpu.VMEM((tm + 2 * SUBLANES, SSM_XBC), F32)],
        compiler_params=_params("parallel", "parallel"),
        name="even_in",
    )(x3, x3, x3, *consts, cos, sa, sb, cos_t, sin_t, *tail)


def _attn_kernel(sink_ref, qt_ref, kc_ref, kp_ref, kn_ref, vc_ref, vp_ref, vn_ref, ga_ref, o_ref,
                 kw_ref, vw_ref, *, tq, nblk):
    n = pl.program_id(1)
    blk = ATTN_BLOCK
    nb = tq // blk
    kw_ref[0:blk] = kp_ref[0]
    kw_ref[blk:blk + tq] = kc_ref[0]
    kw_ref[blk + tq:] = kn_ref[0]
    vw_ref[0] = vp_ref[0, 0]
    vw_ref[1:nb + 1] = vc_ref[0]
    vw_ref[nb + 1] = vn_ref[0, 0]
    kpos = lax.broadcasted_iota(jnp.int32, (blk, blk), 0)
    qpos = lax.broadcasted_iota(jnp.int32, (blk, blk), 1)
    lane = lax.broadcasted_iota(jnp.int32, (1, ATTN_GROUP * blk), 1)
    ones = jnp.ones((HALO, blk), BF16)
    neg = F32(-1e30)

    def body(j, carry):
        qb = n * nb + j
        vp = jnp.concatenate([(kpos >= qpos) & (qb > 0)] * ATTN_GROUP, axis=1)
        vn = jnp.concatenate([(kpos <= qpos) & (qb < nblk - 1)] * ATTN_GROUP, axis=1)
        row0 = pl.multiple_of(j * blk, blk)
        qt = qt_ref[0, j]
        kwin = kw_ref[pl.ds(row0, 3 * blk), :]
        vts = (vw_ref[j], vw_ref[j + 1], vw_ref[j + 2])
        outs = []
        for g in range(ATTN_KV_HEADS):
            hd = slice(g * HEAD_DIM, (g + 1) * HEAD_DIM)
            q4 = jnp.concatenate(
                [qt[(g * ATTN_GROUP + hh) * HEAD_DIM:(g * ATTN_GROUP + hh + 1) * HEAD_DIM] for hh in range(ATTN_GROUP)],
                axis=1)
            kg = kwin[:, hd]
            sp = jnp.where(vp, _dot(kg[0:blk], q4), neg)
            sc = _dot(kg[blk:2 * blk], q4)
            sn = jnp.where(vn, _dot(kg[2 * blk:], q4), neg)
            sk = jnp.full((1, ATTN_GROUP * blk), sink_ref[g * ATTN_GROUP], F32)
            for hh in range(1, ATTN_GROUP):
                sk = jnp.where(lane >= hh * blk, sink_ref[g * ATTN_GROUP + hh], sk)
            sk = sk * LOG2E
            m = jnp.maximum(jnp.max(jnp.maximum(jnp.maximum(sp, sc), sn), axis=0, keepdims=True), sk)
            ot = None
            for v, sb in zip(vts, (sp, sc, sn)):
                ve = jnp.concatenate([v[hd], ones], axis=0)
                part = _dot(ve, jnp.exp2(sb - m).astype(BF16))
                ot = part if ot is None else ot + part
            den = ot[HEAD_DIM:HEAD_DIM + 1] + jnp.exp2(sk - m)
            ot = ot[0:HEAD_DIM] * (1.0 / den)
            ot = jnp.concatenate([ot[:, hh * blk:(hh + 1) * blk] for hh in range(ATTN_GROUP)], axis=0)
            outs.append(ot.T)
        o_all = jnp.concatenate(outs, axis=1)
        ga = ga_ref[0, pl.ds(row0, blk), :].astype(F32)
        o_ref[0, pl.ds(row0, blk), :] = (o_all * _silu(ga)).astype(BF16)
        return carry

    lax.fori_loop(0, nb, body, 0)


def _attention(sink, qt4, k3, vt4, ga3):
    b, l, _ = k3.shape
    tq = min(ATTN_TILE, l)
    per = tq // ATTN_BLOCK
    nblk = l // ATTN_BLOCK
    cur = lambda bi, n: (bi, n, 0)
    prev, nxt = _halo_maps(per, nblk)
    cur4 = lambda bi, n: (bi, n, 0, 0)
    prev4 = lambda bi, n: prev(bi, n) + (0,)
    nxt4 = lambda bi, n: nxt(bi, n) + (0,)
    return pl.pallas_call(
        functools.partial(_attn_kernel, tq=tq, nblk=nblk),
        grid=(b, l // tq),
        in_specs=[pl.BlockSpec(memory_space=pltpu.SMEM),
                  pl.BlockSpec((1, per, ATTN_Q, ATTN_BLOCK), cur4),
                  pl.BlockSpec((1, tq, ATTN_KV), cur), pl.BlockSpec((1, ATTN_BLOCK, ATTN_KV), prev),
                  pl.BlockSpec((1, ATTN_BLOCK, ATTN_KV), nxt),
                  pl.BlockSpec((1, per, ATTN_KV, ATTN_BLOCK), cur4), pl.BlockSpec((1, 1, ATTN_KV, ATTN_BLOCK), prev4),
                  pl.BlockSpec((1, 1, ATTN_KV, ATTN_BLOCK), nxt4),
                  pl.BlockSpec((1, tq, ATTN_Q), cur)],
        out_specs=pl.BlockSpec((1, tq, ATTN_Q), cur),
        out_shape=jax.ShapeDtypeStruct((b, l, ATTN_Q), BF16),
        scratch_shapes=[pltpu.VMEM((tq + 2 * ATTN_BLOCK, ATTN_KV), BF16),
                        pltpu.VMEM((per + 2, ATTN_KV, ATTN_BLOCK), BF16)],
        compiler_params=_params("parallel", "parallel"),
        name="attention",
    )(sink, qt4, k3, k3, k3, vt4, vt4, vt4, ga3)


def _ssd_kernel(*refs, rev):
    if rev:
        (xbc_ref, dt_ref, bias_ref, alog_ref, yf_ref, z_ref, dskip_ref, gn_ref, out_ref, st_ref) = refs
    else:
        (xbc_ref, dt_ref, bias_ref, alog_ref, out_ref, st_ref) = refs
    ch = SSM_CHUNK
    hp = SSM_HEADS // SSM_GROUPS
    gw = hp * SSM_HEAD_DIM
    off = SSM_HEADS if rev else 0

    @pl.when(pl.program_id(1) == 0)
    def _():
        st_ref[...] = jnp.zeros_like(st_ref)

    dt = jax.nn.softplus(dt_ref[0] + bias_ref[...])
    a = -jnp.exp(alog_ref[...]) * LOG2E
    v = dt * a
    ri = lax.broadcasted_iota(jnp.int32, (ch, ch), 0)
    ci = lax.broadcasted_iota(jnp.int32, (ch, ch), 1)
    tri = (ci >= ri) if rev else (ci <= ri)
    trib = jnp.where(tri, 1.0, 0.0).astype(BF16)
    v1 = v.astype(BF16)
    r1 = v - v1.astype(F32)
    v2 = r1.astype(BF16)
    v3 = (r1 - v2.astype(F32)).astype(BF16)
    acs = _dot(trib, v1) + _dot(trib, v2) + _dot(trib, v3)
    tot = acs[0:1, :] if rev else acs[ch - 1:ch, :]
    te = jnp.exp2(tot - acs) * dt
    cdec = jnp.exp2(tot)
    acs_t = acs.T
    arow_t = acs_t - jnp.log(dt.T) * LOG2E
    te_t = te.T
    lo = lax.broadcasted_iota(jnp.int32, (ch, LANES), 1) < SSM_HEAD_DIM
    lo1 = lo[0:1, :]

    xs = xbc_ref[0, :, 0:SSM_INNER]
    ys = []
    for g in range(SSM_GROUPS):
        bg = xbc_ref[0, :, SSM_INNER + g * SSM_STATE:SSM_INNER + (g + 1) * SSM_STATE]
        cg = xbc_ref[0, :, SSM_INNER + (SSM_GROUPS + g) * SSM_STATE:SSM_INNER + (SSM_GROUPS + g + 1) * SSM_STATE]
        cb = _dot_nt(cg, bg)
        bt = bg.astype(F32).T
        st_prev = st_ref[g]
        eacs_x, cdec_x, yd, sn = [], [], [], []
        for k in range(hp // 2):
            h0 = off + g * hp + 2 * k
            xp = xs[:, g * gw + k * LANES:g * gw + (k + 1) * LANES]
            ms, bs, ea = [], [], []
            for e in range(2):
                h = h0 + e
                acol = jnp.broadcast_to(acs[:, h:h + 1], (ch, ch))
                dec = jnp.exp2(jnp.where(tri, acol - arow_t[h:h + 1, :], -jnp.inf))
                ms.append((cb * dec).astype(BF16))
                bs.append((bt * te_t[h:h + 1, :]).astype(BF16))
                ea.append(jnp.exp2(acol))
            zero = jnp.zeros_like(xp)
            rhs = jnp.concatenate([jnp.where(lo, xp, zero), jnp.where(lo, zero, xp)], axis=0)
            yd.append(_dot(jnp.concatenate(ms, axis=1), rhs))
            sn.append(_dot(jnp.concatenate(bs, axis=1), rhs))
            eacs_x.append(jnp.where(lo, ea[0], ea[1]))
            cdec_x.append(jnp.where(lo1, cdec[:, h0:h0 + 1], cdec[:, h0 + 1:h0 + 2]))
        y_off = _dot(cg, st_prev.astype(BF16)) * jnp.concatenate(eacs_x, axis=1)
        st_ref[g] = st_prev * jnp.concatenate(cdec_x, axis=1) + jnp.concatenate(sn, axis=1)
        ys.append(jnp.concatenate(yd, axis=1) + y_off)
    y = jnp.concatenate(ys, axis=1)
    if rev:
        y = y + yf_ref[0] + dskip_ref[...] * xs.astype(F32)
        y = y * _silu(z_ref[0].astype(F32))
        out_ref[0] = _rms(y, gn_ref[...]).astype(BF16)
    else:
        out_ref[0] = y


def _ssd(xbcc3, dt3, bias, alog, rev, extra=()):
    b, l, _ = xbcc3.shape
    nc = l // SSM_CHUNK
    idx = (lambda bi, c: (bi, nc - 1 - c, 0)) if rev else (lambda bi, c: (bi, c, 0))
    blk = lambda w: pl.BlockSpec((1, SSM_CHUNK, w), idx)
    in_specs = [blk(SSM_XBC), blk(LANES), _const_spec(bias.shape), _const_spec(alog.shape)]
    if rev:
        yf, z3, dskip, gn = extra
        in_specs += [blk(SSM_INNER), blk(SSM_INNER), _const_spec(dskip.shape), _const_spec(gn.shape)]
    return pl.pallas_call(
        functools.partial(_ssd_kernel, rev=rev),
        grid=(b, nc),
        in_specs=in_specs,
        out_specs=blk(SSM_INNER),
        out_shape=jax.ShapeDtypeStruct((b, l, SSM_INNER), BF16 if rev else F32),
        scratch_shapes=[pltpu.VMEM((SSM_GROUPS, SSM_STATE, SSM_INNER // SSM_GROUPS), F32)],
        compiler_params=_params("parallel", "arbitrary"),
        name="ssd_bwd" if rev else "ssd_fwd",
    )(xbcc3, dt3, bias, alog, *extra)


def _tail(y, x, p, gpost, wg, wp, gple):
    x1 = x + _rms(y, gpost)
    gate = _sigmoid(_dot(x1.astype(BF16), wg))
    pp = _dot(p.astype(BF16), wp)
    return x1 + _rms(gate * pp, gple)


def _even_out_kernel(oa_ref, ob_ref, x_ref, p_ref, woa_ref, wob_ref, gpost_ref, wg_ref, wp_ref, gple_ref,
                     out_ref):
    y = _dot(oa_ref[...], woa_ref[...]) + _dot(ob_ref[...], wob_ref[...])
    out_ref[...] = _tail(y, x_ref[...], p_ref[...], gpost_ref[...], wg_ref[...], wp_ref[...], gple_ref[...])


def _even_out(oa, ob, x2, p3, layer, woa, wob, gpost, wg, wp, gple, seq_len):
    t = x2.shape[0]
    tm = min(TOKEN_TILE, seq_len)
    row = lambda n: (n, 0)
    consts = (woa, wob, gpost, wg, wp, gple)
    return pl.pallas_call(
        _even_out_kernel,
        grid=(t // tm,),
        in_specs=[pl.BlockSpec((tm, ATTN_Q), row), pl.BlockSpec((tm, SSM_INNER), row),
                  pl.BlockSpec((tm, D_MODEL), row), pl.BlockSpec((None, tm, PLE_DIM), lambda n: (layer, n, 0))]
                 + [_const_spec(c.shape) for c in consts],
        out_specs=pl.BlockSpec((tm, D_MODEL), row),
        out_shape=jax.ShapeDtypeStruct((t, D_MODEL), F32),
        compiler_params=_params("parallel"),
        name="even_out",
    )(oa, ob, x2, p3, *consts)


def _odd_in_kernel(x_ref, g_ref, wa_ref, wb_ref, wg_ref, h_ref, sg_ref):
    u = _rms(x_ref[...], g_ref[...]).astype(BF16)
    a = _dot(u, wa_ref[...])
    b = _dot(u, wb_ref[...])
    h_ref[...] = (a * _sigmoid(b)).astype(BF16)
    sg_ref[...] = _silu(_dot(u, wg_ref[...])).astype(BF16)


def _odd_in(x2, g, wa, wb, wg, seq_len):
    t = x2.shape[0]
    tm = min(TOKEN_TILE, seq_len)
    row = lambda n: (n, 0)
    return pl.pallas_call(
        _odd_in_kernel,
        grid=(t // tm,),
        in_specs=[pl.BlockSpec((tm, D_MODEL), row), _const_spec(g.shape), _const_spec(wa.shape),
                  _const_spec(wb.shape), _const_spec(wg.shape)],
        out_specs=[pl.BlockSpec((tm, CONV_INNER), row), pl.BlockSpec((tm, CONV_INNER), row)],
        out_shape=[jax.ShapeDtypeStruct((t, CONV_INNER), BF16), jax.ShapeDtypeStruct((t, CONV_INNER), BF16)],
        compiler_params=_params("parallel"),
        name="odd_in",
    )(x2, g, wa, wb, wg)


def _odd_out_kernel(hc_ref, hp_ref, hn_ref, sg_ref, x_ref, p_ref, cw_ref, cb_ref, lng_ref, lnb_ref, wo_ref,
                    gpost_ref, wg_ref, wp_ref, gple_ref, out_ref, pad_ref, sh_ref, wb_ref, cv_ref, *, tc, ntile):
    n = pl.program_id(1)
    pad_ref[0:HALO] = jnp.where(n > 0, hp_ref[0].astype(F32), 0.0)
    pad_ref[HALO:HALO + tc] = hc_ref[0].astype(F32)
    pad_ref[HALO + tc:] = jnp.where(n < ntile - 1, hn_ref[0].astype(F32), 0.0)
    half = CONV_WIDTH // 2
    rows = CONV_ROWS
    span = sh_ref.shape[1]
    for c in range(SUBLANES):
        sh_ref[c] = pad_ref[pl.ds(c, span), :]
    for k in range(CONV_WIDTH):
        wb_ref[k] = jnp.broadcast_to(cw_ref[k:k + 1, :], (SUBLANES, CONV_INNER))

    def body(i, carry):
        r0 = pl.multiple_of(i * rows, rows)
        for l0 in range(0, CONV_INNER, CONV_LANES):
            ls = slice(l0, l0 + CONV_LANES)
            accs = [jnp.zeros((SUBLANES, CONV_LANES), F32) + cb_ref[:, ls] for _ in range(rows // SUBLANES)]
            for k in range(CONV_WIDTH):
                o = HALO - half + k
                w = wb_ref[k, :, ls]
                for r in range(rows // SUBLANES):
                    a0 = r0 + o - o % SUBLANES + r * SUBLANES
                    accs[r] = accs[r] + sh_ref[o % SUBLANES, pl.ds(a0, SUBLANES), ls] * w
            for r in range(rows // SUBLANES):
                cv_ref[pl.ds(r0 + r * SUBLANES, SUBLANES), ls] = accs[r]
        return carry

    lax.fori_loop(0, tc // rows, body, 0)
    cv = cv_ref[...]
    mu = jnp.mean(cv, axis=-1, keepdims=True)
    cen = cv - mu
    var = jnp.mean(cen * cen, axis=-1, keepdims=True)
    hn = _silu(cen * lax.rsqrt(var + NORM_EPS) * lng_ref[...] + lnb_ref[...])
    hm = (hn * sg_ref[0].astype(F32)).astype(BF16)
    y = _dot(hm, wo_ref[...])
    out_ref[0] = _tail(y, x_ref[0], p_ref[0], gpost_ref[...], wg_ref[...], wp_ref[...], gple_ref[...])


def _odd_out(h3, sg3, x3, p4, layer, cw, cb, lng, lnb, wo, gpost, wg, wp, gple):
    b, l, _ = h3.shape
    tc = min(CONV_TILE, l)
    ntile = l // tc
    cur = lambda bi, n: (bi, n, 0)
    prev, nxt = _halo_maps(tc // HALO, l // HALO)
    consts = (cw, cb, lng, lnb, wo, gpost, wg, wp, gple)
    return pl.pallas_call(
        functools.partial(_odd_out_kernel, tc=tc, ntile=ntile),
        grid=(b, ntile),
        in_specs=[pl.BlockSpec((1, tc, CONV_INNER), cur), pl.BlockSpec((1, HALO, CONV_INNER), prev),
                  pl.BlockSpec((1, HALO, CONV_INNER), nxt), pl.BlockSpec((1, tc, CONV_INNER), cur),
                  pl.BlockSpec((1, tc, D_MODEL), cur),
                  pl.BlockSpec((None, 1, tc, PLE_DIM), lambda bi, n: (layer, bi, n, 0))]
                 + [_const_spec(c.shape) for c in consts],
        out_specs=pl.BlockSpec((1, tc, D_MODEL), cur),
        out_shape=jax.ShapeDtypeStruct((b, l, D_MODEL), F32),
        scratch_shapes=[pltpu.VMEM((tc + 2 * HALO, CONV_INNER), F32),
                        pltpu.VMEM((SUBLANES, tc + 2 * HALO - SUBLANES, CONV_INNER), F32),
                        pltpu.VMEM((CONV_WIDTH, SUBLANES, CONV_INNER), F32),
                        pltpu.VMEM((tc, CONV_INNER), F32)],
        compiler_params=_params("parallel", "parallel"),
        name="odd_out",
    )(h3, h3, h3, sg3, x3, p4, *consts)


def _rope_tables(length):
    inv = 1.0 / (jnp.float32(ROPE_THETA) ** (jnp.arange(0, ROT_DIM, 2, dtype=F32) / ROT_DIM))
    ang = jnp.arange(length, dtype=F32)[:, None] * inv[None, :]
    cos, sin = jnp.cos(ang), jnp.sin(ang)
    m = jnp.arange(LANES) % HEAD_DIM
    idx = m % ROT_HALF
    cos_l = jnp.where(m < ROT_DIM, cos[:, idx], 1.0)
    sin_l = sin[:, idx]
    sa = jnp.where((m >= ROT_HALF) & (m < ROT_DIM), sin_l, 0.0)
    sb = jnp.where(m < ROT_HALF, -sin_l, 0.0)
    return cos_l, sa, sb, cos.T, sin.T


def _row(v):
    return v.reshape(1, -1).astype(F32)


def _pad_rows(w, rows):
    return jnp.concatenate([w, jnp.zeros((rows - w.shape[0],) + w.shape[1:], w.dtype)], axis=0)


def _pad_lanes(v, lanes=LANES):
    v = v.reshape(1, -1).astype(F32)
    return jnp.concatenate([v, jnp.zeros((1, lanes - v.shape[1]), F32)], axis=1)


def _even_weights(j, ev_w_in, ev_w_out, attn_sink, ssm_conv_w, ssm_conv_b, ssm_dt_bias, ssm_a_log, ssm_d,
                  ssm_norm):
    w = ev_w_in[j].astype(BF16)
    c0 = 0
    cols = []
    for width in (ATTN_Q, ATTN_KV, ATTN_KV, ATTN_Q, SSM_XBC, SSM_INNER, 2 * SSM_HEADS):
        cols.append(w[:, c0:c0 + width])
        c0 += width
    wq, wk, wv, wga, wxbc, wz, wdt = cols
    wdt = jnp.concatenate([wdt, jnp.zeros((D_MODEL, LANES - 2 * SSM_HEADS), BF16)], axis=1)
    wo = ev_w_out[j].astype(BF16)
    return dict(
        wqt=wq.T, wk=wk, wvt=wv.T, wga=wga, wxbc=wxbc, wz=wz, wdt=wdt, woa=wo[:ATTN_Q], wob=wo[ATTN_Q:],
        sink=attn_sink[j].astype(F32),
        conv_w=_pad_rows(ssm_conv_w[j].astype(F32), SUBLANES), conv_b=_row(ssm_conv_b[j]),
        dt_bias=_pad_lanes(ssm_dt_bias[j]), a_log=_pad_lanes(ssm_a_log[j]),
        dskip=_row(jnp.repeat(ssm_d[j].astype(F32), SSM_HEAD_DIM)), ssm_norm=_row(ssm_norm[j]))


def _odd_weights(j, od_w_in, od_conv_w, od_conv_b, od_ln_g, od_ln_b, od_w_out):
    w = od_w_in[j].astype(BF16)
    return dict(
        wa=w[:, :CONV_INNER], wb=w[:, CONV_INNER:2 * CONV_INNER], wg=w[:, 2 * CONV_INNER:],
        conv_w=_pad_rows(od_conv_w[j].astype(F32), 4 * SUBLANES), conv_b=_row(od_conv_b[j]),
        ln_g=_row(od_ln_g[j]), ln_b=_row(od_ln_b[j]), wo=od_w_out[j].astype(BF16))


def _trunk(x, p, layers, rope):
    b, l, _ = x.shape
    t = b * l
    flat = lambda a: a.reshape(t, a.shape[-1])
    seq = lambda a: a.reshape(b, l, a.shape[-1])
    p3 = p.reshape(p.shape[0], t, PLE_DIM)
    for i, lw in enumerate(layers):
        common = (lw["gpost"], lw["ple_wg"], lw["ple_wp"], lw["gple"])
        if i % 2 == 0:
            qt, k, vt, ga, xbcc, z, dt = _even_in(x, lw["gpre"], lw, rope)
            o_attn = _attention(lw["sink"], qt, k, vt, ga)
            y_f = _ssd(xbcc, dt, lw["dt_bias"], lw["a_log"], rev=False)
            o_ssm = _ssd(xbcc, dt, lw["dt_bias"], lw["a_log"], rev=True,
                         extra=(y_f, z, lw["dskip"], lw["ssm_norm"]))
            x = seq(_even_out(flat(o_attn), flat(o_ssm), flat(x), p3, i, lw["woa"], lw["wob"], *common, l))
        else:
            h, sg = _odd_in(flat(x), lw["gpre"], lw["wa"], lw["wb"], lw["wg"], l)
            x = _odd_out(seq(h), seq(sg), x, p, i, lw["conv_w"], lw["conv_b"], lw["ln_g"], lw["ln_b"], lw["wo"],
                         *common)
    return x


def kernel(x_prompt, x_sample, p_prompt, p_sample, norm_pre, norm_post, ple_w_gate, ple_w_proj, ple_norm, ev_w_in, ev_w_out, attn_sink, ssm_conv_w, ssm_conv_b, ssm_dt_bias, ssm_a_log, ssm_d, ssm_norm, od_w_in, od_conv_w, od_conv_b, od_ln_g, od_ln_b, od_w_out):
    layers = []
    for i in range(DEPTH):
        j = i // 2
        if i % 2 == 0:
            lw = _even_weights(j, ev_w_in, ev_w_out, attn_sink, ssm_conv_w, ssm_conv_b, ssm_dt_bias, ssm_a_log,
                               ssm_d, ssm_norm)
        else:
            lw = _odd_weights(j, od_w_in, od_conv_w, od_conv_b, od_ln_g, od_ln_b, od_w_out)
        lw.update(gpre=_row(norm_pre[i]), gpost=_row(norm_post[i]), ple_wg=ple_w_gate[i].astype(BF16),
                  ple_wp=ple_w_proj[i].astype(BF16), gple=_row(ple_norm[i]))
        layers.append(lw)
    y_prompt = _trunk(x_prompt, p_prompt, layers, _rope_tables(x_prompt.shape[1]))
    y_sample = _trunk(x_sample, p_sample, layers, _rope_tables(x_sample.shape[1]))
    return (y_prompt, y_sample)
```

```python
import functools

import jax
import jax.numpy as jnp
from jax import lax
from jax.experimental import pallas as pl
from jax.experimental.pallas import tpu as pltpu

F32 = jnp.float32
BF16 = jnp.bfloat16

D_MODEL = 1024
DEPTH = 4
PLE_DIM = 256
NORM_EPS = 1e-6
ATTN_HEADS = 16
ATTN_KV_HEADS = 4
HEAD_DIM = 64
ATTN_GROUP = ATTN_HEADS // ATTN_KV_HEADS
ATTN_BLOCK = 128
ROPE_THETA = 500000.0
ROT_DIM = HEAD_DIM // 4
ROT_HALF = ROT_DIM // 2
ATTN_Q = ATTN_HEADS * HEAD_DIM
ATTN_KV = ATTN_KV_HEADS * HEAD_DIM
SSM_INNER = D_MODEL
SSM_HEAD_DIM = 64
SSM_HEADS = SSM_INNER // SSM_HEAD_DIM
SSM_GROUPS = 2
SSM_STATE = 128
SSM_CONV = 5
SSM_CHUNK = 128
SSM_BC = 2 * SSM_GROUPS * SSM_STATE
SSM_XBC = SSM_INNER + SSM_BC
CONV_INNER = D_MODEL
CONV_WIDTH = 31

LANES = 128
SUBLANES = 8
HALO = 16
VMEM_LIMIT = 56 * 1024 * 1024

TOKEN_TILE = 512
WIDE_TILE = 1024
ATTN_TILE = 512
CONV_TILE = 256
SSD_STEP = 4
CONV_ROWS = 32
CONV_LANES = 512

NT_DIMS = (((1,), (1,)), ((), ()))
LOG2E = 1.4426950408889634


def _params(*sem):
    return pltpu.CompilerParams(dimension_semantics=sem, vmem_limit_bytes=VMEM_LIMIT)


def _const_spec(shape):
    nd = len(shape)
    return pl.BlockSpec(shape, lambda *_: (0,) * nd, pipeline_mode=pl.Buffered(1))


def _rms(x, g):
    ms = jnp.mean(x * x, axis=-1, keepdims=True)
    return x * lax.rsqrt(ms + NORM_EPS) * g


def _sigmoid(x):
    return jax.nn.sigmoid(x)


def _silu(x):
    return x * _sigmoid(x)


def _dot(a, b):
    return jnp.dot(a, b, preferred_element_type=F32)


def _dot_nt(a, b):
    return lax.dot_general(a, b, NT_DIMS, preferred_element_type=F32)


def _halo_maps(per, nh):
    prev = lambda bi, n: (bi, jnp.maximum(n * per - 1, 0), 0)
    nxt = lambda bi, n: (bi, jnp.minimum((n + 1) * per, nh - 1), 0)
    return prev, nxt


def _even_in_kernel(xc_ref, xp_ref, xn_ref, g_ref, wqt_ref, wk_ref, wvt_ref, wga_ref, wxbc_ref, wz_ref, wdt_ref,
                    cos_ref, sa_ref, sb_ref, cost_ref, sint_ref, cw_ref, cb_ref,
                    qt_ref, k_ref, vt_ref, ga_ref, xbc_ref, z_ref, dt_ref, pad_ref, *, tm, ntile):
    n = pl.program_id(1)
    g = g_ref[...]
    u = _rms(xc_ref[0], g).astype(BF16)

    qt = _dot_nt(wqt_ref[...], u)
    ct = cost_ref[...]
    st = sint_ref[...]
    parts = []
    for h in range(ATTN_HEADS):
        b0 = h * HEAD_DIM
        r1 = qt[b0:b0 + ROT_HALF]
        r2 = qt[b0 + ROT_HALF:b0 + ROT_DIM]
        parts += [r1 * ct - r2 * st, r2 * ct + r1 * st, qt[b0 + ROT_DIM:b0 + HEAD_DIM]]
    scale = HEAD_DIM ** -0.5 * LOG2E
    qt = (jnp.concatenate(parts, axis=0) * scale).astype(BF16)
    vt = _dot_nt(wvt_ref[...], u).astype(BF16)
    for j in range(tm // ATTN_BLOCK):
        sl = slice(j * ATTN_BLOCK, (j + 1) * ATTN_BLOCK)
        qt_ref[0, j] = qt[:, sl]
        vt_ref[0, j] = vt[:, sl]

    cos = cos_ref[...]
    sa = sa_ref[...]
    sb = sb_ref[...]
    k = _dot(u, wk_ref[...])
    for j in range(ATTN_KV // LANES):
        sl = slice(j * LANES, (j + 1) * LANES)
        t = k[:, sl]
        k_ref[0, :, sl] = (t * cos + pltpu.roll(t, ROT_HALF, 1) * sa
                           + pltpu.roll(t, LANES - ROT_HALF, 1) * sb).astype(BF16)

    ga_ref[0] = _dot(u, wga_ref[...]).astype(BF16)
    z_ref[0] = _dot(u, wz_ref[...]).astype(BF16)
    dt_ref[0] = _dot(u, wdt_ref[...])

    xh = jnp.concatenate([xp_ref[0], xn_ref[0]], axis=0)
    uh = _rms(xh, g).astype(BF16)
    hal = _dot(uh, wxbc_ref[...])
    pad_ref[0:SUBLANES] = jnp.where(n > 0, hal[0:SUBLANES], 0.0)
    pad_ref[SUBLANES:SUBLANES + tm] = _dot(u, wxbc_ref[...])
    pad_ref[SUBLANES + tm:] = jnp.where(n < ntile - 1, hal[SUBLANES:], 0.0)
    half = SSM_CONV // 2
    acc = jnp.zeros((tm, SSM_XBC), F32) + cb_ref[...]
    for kk in range(SSM_CONV):
        acc = acc + pad_ref[pl.ds(SUBLANES - half + kk, tm), :] * cw_ref[kk:kk + 1, :]
    xbc_ref[0] = _silu(acc).astype(BF16)


def _even_in(x3, g, lw, rope):
    b, l, _ = x3.shape
    tm = min(TOKEN_TILE, l)
    ntile = l // tm
    nb = tm // ATTN_BLOCK
    cos, sa, sb, cos_t, sin_t = rope
    cur = lambda bi, n: (bi, n, 0)
    prev, nxt = _halo_maps(tm // SUBLANES, l // SUBLANES)
    cur4 = lambda bi, n: (bi, n, 0, 0)
    pos = lambda bi, n: (n, 0)
    pos_t = lambda bi, n: (0, n)
    consts = (g, lw["wqt"], lw["wk"], lw["wvt"], lw["wga"], lw["wxbc"], lw["wz"], lw["wdt"])
    tail = (lw["conv_w"], lw["conv_b"])
    out_shape = [jax.ShapeDtypeStruct((b, l // ATTN_BLOCK, ATTN_Q, ATTN_BLOCK), BF16),
                 jax.ShapeDtypeStruct((b, l, ATTN_KV), BF16),
                 jax.ShapeDtypeStruct((b, l // ATTN_BLOCK, ATTN_KV, ATTN_BLOCK), BF16),
                 jax.ShapeDtypeStruct((b, l, ATTN_Q), BF16),
                 jax.ShapeDtypeStruct((b, l, SSM_XBC), BF16),
                 jax.ShapeDtypeStruct((b, l, SSM_INNER), BF16),
                 jax.ShapeDtypeStruct((b, l, LANES), F32)]
    out_specs = [pl.BlockSpec((1, nb, ATTN_Q, ATTN_BLOCK), cur4), pl.BlockSpec((1, tm, ATTN_KV), cur),
                 pl.BlockSpec((1, nb, ATTN_KV, ATTN_BLOCK), cur4), pl.BlockSpec((1, tm, ATTN_Q), cur),
                 pl.BlockSpec((1, tm, SSM_XBC), cur), pl.BlockSpec((1, tm, SSM_INNER), cur),
                 pl.BlockSpec((1, tm, LANES), cur)]
    return pl.pallas_call(
        functools.partial(_even_in_kernel, tm=tm, ntile=ntile),
        grid=(b, ntile),
        in_specs=[pl.BlockSpec((1, tm, D_MODEL), cur), pl.BlockSpec((1, SUBLANES, D_MODEL), prev),
                  pl.BlockSpec((1, SUBLANES, D_MODEL), nxt)]
                 + [_const_spec(c.shape) for c in consts]
                 + [pl.BlockSpec((tm, LANES), pos)] * 3 + [pl.BlockSpec((ROT_HALF, tm), pos_t)] * 2
                 + [_const_spec(c.shape) for c in tail],
        out_specs=out_specs,
        out_shape=out_shape,
        scratch_shapes=[pltpu.VMEM((tm + 2 * SUBLANES, SSM_XBC), F32)],
        compiler_params=_params("parallel", "parallel"),
        name="even_in",
    )(x3, x3, x3, *consts, cos, sa, sb, cos_t, sin_t, *tail)


def _attn_kernel(sink_ref, qt_ref, kc_ref, kp_ref, kn_ref, vc_ref, vp_ref, vn_ref, ga_ref, o_ref,
                 kw_ref, vw_ref, *, tq, nblk):
    n = pl.program_id(1)
    blk = ATTN_BLOCK
    nb = tq // blk
    kw_ref[0:blk] = kp_ref[0]
    kw_ref[blk:blk + tq] = kc_ref[0]
    kw_ref[blk + tq:] = kn_ref[0]
    vw_ref[0] = vp_ref[0, 0]
    vw_ref[1:nb + 1] = vc_ref[0]
    vw_ref[nb + 1] = vn_ref[0, 0]
    kpos = lax.broadcasted_iota(jnp.int32, (blk, blk), 0)
    qpos = lax.broadcasted_iota(jnp.int32, (blk, blk), 1)
    lane = lax.broadcasted_iota(jnp.int32, (1, ATTN_GROUP * blk), 1)
    ones = jnp.ones((HALO, blk), BF16)
    neg = F32(-1e30)

    def body(j, carry):
        qb = n * nb + j
        vp = jnp.concatenate([(kpos >= qpos) & (qb > 0)] * ATTN_GROUP, axis=1)
        vn = jnp.concatenate([(kpos <= qpos) & (qb < nblk - 1)] * ATTN_GROUP, axis=1)
        row0 = pl.multiple_of(j * blk, blk)
        qt = qt_ref[0, j]
        kwin = kw_ref[pl.ds(row0, 3 * blk), :]
        vts = (vw_ref[j], vw_ref[j + 1], vw_ref[j + 2])

        def scores(g):
            q4 = jnp.concatenate(
                [qt[(g * ATTN_GROUP + hh) * HEAD_DIM:(g * ATTN_GROUP + hh + 1) * HEAD_DIM] for hh in range(ATTN_GROUP)],
                axis=1)
            kg = kwin[:, g * HEAD_DIM:(g + 1) * HEAD_DIM]
            return (jnp.where(vp, _dot(kg[0:blk], q4), neg),
                    _dot(kg[blk:2 * blk], q4),
                    jnp.where(vn, _dot(kg[2 * blk:], q4), neg))

        pend = [scores(g) for g in range(ATTN_KV_HEADS)]
        outs = []
        for g in range(ATTN_KV_HEADS):
            hd = slice(g * HEAD_DIM, (g + 1) * HEAD_DIM)
            sp, sc, sn = pend[g]
            sk = jnp.full((1, ATTN_GROUP * blk), sink_ref[g * ATTN_GROUP], F32)
            for hh in range(1, ATTN_GROUP):
                sk = jnp.where(lane >= hh * blk, sink_ref[g * ATTN_GROUP + hh], sk)
            sk = sk * LOG2E
            m = jnp.maximum(jnp.max(jnp.maximum(jnp.maximum(sp, sc), sn), axis=0, keepdims=True), sk)
            ot = None
            for v, sb in zip(vts, (sp, sc, sn)):
                ve = jnp.concatenate([v[hd], ones], axis=0)
                part = _dot(ve, jnp.exp2(sb - m).astype(BF16))
                ot = part if ot is None else ot + part
            den = ot[HEAD_DIM:HEAD_DIM + 1] + jnp.exp2(sk - m)
            ot = ot[0:HEAD_DIM] * (1.0 / den)
            ot = jnp.concatenate([ot[:, hh * blk:(hh + 1) * blk] for hh in range(ATTN_GROUP)], axis=0)
            outs.append(ot.T)
        o_all = jnp.concatenate(outs, axis=1)
        ga = ga_ref[0, pl.ds(row0, blk), :].astype(F32)
        o_ref[0, pl.ds(row0, blk), :] = (o_all * _silu(ga)).astype(BF16)
        return carry

    lax.fori_loop(0, nb, body, 0)


def _attention(sink, qt4, k3, vt4, ga3):
    b, l, _ = k3.shape
    tq = min(ATTN_TILE, l)
    per = tq // ATTN_BLOCK
    nblk = l // ATTN_BLOCK
    cur = lambda bi, n: (bi, n, 0)
    prev, nxt = _halo_maps(per, nblk)
    cur4 = lambda bi, n: (bi, n, 0, 0)
    prev4 = lambda bi, n: prev(bi, n) + (0,)
    nxt4 = lambda bi, n: nxt(bi, n) + (0,)
    return pl.pallas_call(
        functools.partial(_attn_kernel, tq=tq, nblk=nblk),
        grid=(b, l // tq),
        in_specs=[pl.BlockSpec(memory_space=pltpu.SMEM),
                  pl.BlockSpec((1, per, ATTN_Q, ATTN_BLOCK), cur4),
                  pl.BlockSpec((1, tq, ATTN_KV), cur), pl.BlockSpec((1, ATTN_BLOCK, ATTN_KV), prev),
                  pl.BlockSpec((1, ATTN_BLOCK, ATTN_KV), nxt),
                  pl.BlockSpec((1, per, ATTN_KV, ATTN_BLOCK), cur4), pl.BlockSpec((1, 1, ATTN_KV, ATTN_BLOCK), prev4),
                  pl.BlockSpec((1, 1, ATTN_KV, ATTN_BLOCK), nxt4),
                  pl.BlockSpec((1, tq, ATTN_Q), cur)],
        out_specs=pl.BlockSpec((1, tq, ATTN_Q), cur),
        out_shape=jax.ShapeDtypeStruct((b, l, ATTN_Q), BF16),
        scratch_shapes=[pltpu.VMEM((tq + 2 * ATTN_BLOCK, ATTN_KV), BF16),
                        pltpu.VMEM((per + 2, ATTN_KV, ATTN_BLOCK), BF16)],
        compiler_params=_params("parallel", "parallel"),
        name="attention",
    )(sink, qt4, k3, k3, k3, vt4, vt4, vt4, ga3)


def _ssd_decay(dtr, bias, alog, rev):
    ch = SSM_CHUNK
    dt = jax.nn.softplus(dtr + bias)
    a = -jnp.exp(alog) * LOG2E
    v = dt * a
    ri = lax.broadcasted_iota(jnp.int32, (ch, ch), 0)
    ci = lax.broadcasted_iota(jnp.int32, (ch, ch), 1)
    tri = (ci >= ri) if rev else (ci <= ri)
    trib = jnp.where(tri, 1.0, 0.0).astype(BF16)
    v1 = v.astype(BF16)
    r1 = v - v1.astype(F32)
    v2 = r1.astype(BF16)
    v3 = (r1 - v2.astype(F32)).astype(BF16)
    acs = _dot(trib, v1) + _dot(trib, v2) + _dot(trib, v3)
    tot = acs[0:1, :] if rev else acs[ch - 1:ch, :]
    te = jnp.exp2(tot - acs) * dt
    cdec = jnp.exp2(tot)
    acs_t = acs.T
    arow_t = acs_t - jnp.log(dt.T) * LOG2E
    return acs, arow_t, te.T, cdec, tri


def _ssd_local(xbc, decay, rev):
    ch = SSM_CHUNK
    hp = SSM_HEADS // SSM_GROUPS
    gw = hp * SSM_HEAD_DIM
    off = SSM_HEADS if rev else 0
    acs, arow_t, te_t, cdec, tri = decay
    lo = lax.broadcasted_iota(jnp.int32, (ch, LANES), 1) < SSM_HEAD_DIM
    lo1 = lo[0:1, :]

    xs = xbc[:, 0:SSM_INNER]
    groups = []
    for g in range(SSM_GROUPS):
        bg = xbc[:, SSM_INNER + g * SSM_STATE:SSM_INNER + (g + 1) * SSM_STATE]
        cg = xbc[:, SSM_INNER + (SSM_GROUPS + g) * SSM_STATE:SSM_INNER + (SSM_GROUPS + g + 1) * SSM_STATE]
        cb = _dot_nt(cg, bg)
        bt = bg.astype(F32).T
        eacs_x, cdec_x, yd, sn = [], [], [], []
        for k in range(hp // 2):
            h0 = off + g * hp + 2 * k
            xp = xs[:, g * gw + k * LANES:g * gw + (k + 1) * LANES]
            ms, bs, ea = [], [], []
            for e in range(2):
                h = h0 + e
                acol = jnp.broadcast_to(acs[:, h:h + 1], (ch, ch))
                dec = jnp.exp2(jnp.where(tri, acol - arow_t[h:h + 1, :], -jnp.inf))
                ms.append((cb * dec).astype(BF16))
                bs.append((bt * te_t[h:h + 1, :]).astype(BF16))
                ea.append(jnp.exp2(acol))
            zero = jnp.zeros_like(xp)
            rhs = jnp.concatenate([jnp.where(lo, xp, zero), jnp.where(lo, zero, xp)], axis=0)
            yd.append(_dot(jnp.concatenate(ms, axis=1), rhs))
            sn.append(_dot(jnp.concatenate(bs, axis=1), rhs))
            eacs_x.append(jnp.where(lo, ea[0], ea[1]))
            cdec_x.append(jnp.where(lo1, cdec[:, h0:h0 + 1], cdec[:, h0 + 1:h0 + 2]))
        groups.append((cg, jnp.concatenate(yd, axis=1), jnp.concatenate(sn, axis=1),
                       jnp.concatenate(eacs_x, axis=1), jnp.concatenate(cdec_x, axis=1)))
    return xs, groups


def _ssd_kernel(*refs, rev, nsub):
    if rev:
        (xbc_ref, dt_ref, bias_ref, alog_ref, yf_ref, z_ref, dskip_ref, gn_ref, out_ref, st_ref) = refs
    else:
        (xbc_ref, dt_ref, bias_ref, alog_ref, out_ref, st_ref) = refs
    ch = SSM_CHUNK

    @pl.when(pl.program_id(1) == 0)
    def _():
        st_ref[...] = jnp.zeros_like(st_ref)

    order = list(range(nsub - 1, -1, -1) if rev else range(nsub))
    rows = [slice(sc * ch, (sc + 1) * ch) for sc in range(nsub)]
    decay = {sc: _ssd_decay(dt_ref[0, rows[sc], :], bias_ref[...], alog_ref[...], rev) for sc in order}
    local = {sc: _ssd_local(xbc_ref[0, rows[sc], :], decay[sc], rev) for sc in order}
    for sc in order:
        xs, groups = local[sc]
        ys = []
        for g, (cg, yd, sn, eacs_x, cdec_x) in enumerate(groups):
            st_prev = st_ref[g]
            ys.append(yd + _dot(cg, st_prev.astype(BF16)) * eacs_x)
            st_ref[g] = st_prev * cdec_x + sn
        y = jnp.concatenate(ys, axis=1)
        if rev:
            y = y + yf_ref[0, rows[sc], :] + dskip_ref[...] * xs.astype(F32)
            y = y * _silu(z_ref[0, rows[sc], :].astype(F32))
            out_ref[0, rows[sc], :] = _rms(y, gn_ref[...]).astype(BF16)
        else:
            out_ref[0, rows[sc], :] = y


def _ssd(xbcc3, dt3, bias, alog, rev, extra=()):
    b, l, _ = xbcc3.shape
    nsub = min(SSD_STEP, l // SSM_CHUNK)
    nc = l // (nsub * SSM_CHUNK)
    idx = (lambda bi, c: (bi, nc - 1 - c, 0)) if rev else (lambda bi, c: (bi, c, 0))
    blk = lambda w: pl.BlockSpec((1, nsub * SSM_CHUNK, w), idx)
    in_specs = [blk(SSM_XBC), blk(LANES), _const_spec(bias.shape), _const_spec(alog.shape)]
    if rev:
        yf, z3, dskip, gn = extra
        in_specs += [blk(SSM_INNER), blk(SSM_INNER), _const_spec(dskip.shape), _const_spec(gn.shape)]
    return pl.pallas_call(
        functools.partial(_ssd_kernel, rev=rev, nsub=nsub),
        grid=(b, nc),
        in_specs=in_specs,
        out_specs=blk(SSM_INNER),
        out_shape=jax.ShapeDtypeStruct((b, l, SSM_INNER), BF16 if rev else F32),
        scratch_shapes=[pltpu.VMEM((SSM_GROUPS, SSM_STATE, SSM_INNER // SSM_GROUPS), F32)],
        compiler_params=_params("parallel", "arbitrary"),
        name="ssd_bwd" if rev else "ssd_fwd",
    )(xbcc3, dt3, bias, alog, *extra)


def _tail(y, x, p, gpost, wg, wp, gple):
    x1 = x + _rms(y, gpost)
    gate = _sigmoid(_dot(x1.astype(BF16), wg))
    pp = _dot(p.astype(BF16), wp)
    return x1 + _rms(gate * pp, gple)


def _even_out_kernel(oa_ref, ob_ref, x_ref, p_ref, woa_ref, wob_ref, gpost_ref, wg_ref, wp_ref, gple_ref,
                     out_ref):
    y = _dot(oa_ref[...], woa_ref[...]) + _dot(ob_ref[...], wob_ref[...])
    out_ref[...] = _tail(y, x_ref[...], p_ref[...], gpost_ref[...], wg_ref[...], wp_ref[...], gple_ref[...])


def _even_out(oa, ob, x2, p3, layer, woa, wob, gpost, wg, wp, gple, seq_len):
    t = x2.shape[0]
    tm = min(WIDE_TILE, seq_len)
    row = lambda n: (n, 0)
    consts = (woa, wob, gpost, wg, wp, gple)
    return pl.pallas_call(
        _even_out_kernel,
        grid=(t // tm,),
        in_specs=[pl.BlockSpec((tm, ATTN_Q), row), pl.BlockSpec((tm, SSM_INNER), row),
                  pl.BlockSpec((tm, D_MODEL), row), pl.BlockSpec((None, tm, PLE_DIM), lambda n: (layer, n, 0))]
                 + [_const_spec(c.shape) for c in consts],
        out_specs=pl.BlockSpec((tm, D_MODEL), row),
        out_shape=jax.ShapeDtypeStruct((t, D_MODEL), F32),
        compiler_params=_params("parallel"),
        name="even_out",
    )(oa, ob, x2, p3, *consts)


def _odd_in_kernel(x_ref, g_ref, wa_ref, wb_ref, wg_ref, h_ref, sg_ref):
    u = _rms(x_ref[...], g_ref[...]).astype(BF16)
    a = _dot(u, wa_ref[...])
    b = _dot(u, wb_ref[...])
    h_ref[...] = (a * _sigmoid(b)).astype(BF16)
    sg_ref[...] = _silu(_dot(u, wg_ref[...])).astype(BF16)


def _odd_in(x2, g, wa, wb, wg, seq_len):
    t = x2.shape[0]
    tm = min(WIDE_TILE, seq_len)
    row = lambda n: (n, 0)
    return pl.pallas_call(
        _odd_in_kernel,
        grid=(t // tm,),
        in_specs=[pl.BlockSpec((tm, D_MODEL), row), _const_spec(g.shape), _const_spec(wa.shape),
                  _const_spec(wb.shape), _const_spec(wg.shape)],
        out_specs=[pl.BlockSpec((tm, CONV_INNER), row), pl.BlockSpec((tm, CONV_INNER), row)],
        out_shape=[jax.ShapeDtypeStruct((t, CONV_INNER), BF16), jax.ShapeDtypeStruct((t, CONV_INNER), BF16)],
        compiler_params=_params("parallel"),
        name="odd_in",
    )(x2, g, wa, wb, wg)


def _odd_out_kernel(hc_ref, hp_ref, hn_ref, sg_ref, x_ref, p_ref, cw_ref, cb_ref, lng_ref, lnb_ref, wo_ref,
                    gpost_ref, wg_ref, wp_ref, gple_ref, out_ref, pad_ref, sh_ref, wb_ref, cv_ref, *, tc, ntile):
    n = pl.program_id(1)
    pad_ref[0:HALO] = jnp.where(n > 0, hp_ref[0].astype(F32), 0.0)
    pad_ref[HALO:HALO + tc] = hc_ref[0].astype(F32)
    pad_ref[HALO + tc:] = jnp.where(n < ntile - 1, hn_ref[0].astype(F32), 0.0)
    half = CONV_WIDTH // 2
    rows = CONV_ROWS
    span = sh_ref.shape[1]
    for c in range(SUBLANES):
        sh_ref[c] = pad_ref[pl.ds(c, span), :]
    for k in range(CONV_WIDTH):
        wb_ref[k] = jnp.broadcast_to(cw_ref[k:k + 1, :], (SUBLANES, CONV_INNER))

    def body(i, carry):
        r0 = pl.multiple_of(i * rows, rows)
        for l0 in range(0, CONV_INNER, CONV_LANES):
            ls = slice(l0, l0 + CONV_LANES)
            accs = [jnp.zeros((SUBLANES, CONV_LANES), F32) + cb_ref[:, ls] for _ in range(rows // SUBLANES)]
            for k in range(CONV_WIDTH):
                o = HALO - half + k
                w = wb_ref[k, :, ls]
                for r in range(rows // SUBLANES):
                    a0 = r0 + o - o % SUBLANES + r * SUBLANES
                    accs[r] = accs[r] + sh_ref[o % SUBLANES, pl.ds(a0, SUBLANES), ls] * w
            for r in range(rows // SUBLANES):
                cv_ref[pl.ds(r0 + r * SUBLANES, SUBLANES), ls] = accs[r]
        return carry

    lax.fori_loop(0, tc // rows, body, 0)
    cv = cv_ref[...]
    mu = jnp.mean(cv, axis=-1, keepdims=True)
    cen = cv - mu
    var = jnp.mean(cen * cen, axis=-1, keepdims=True)
    hn = _silu(cen * lax.rsqrt(var + NORM_EPS) * lng_ref[...] + lnb_ref[...])
    hm = (hn * sg_ref[0].astype(F32)).astype(BF16)
    y = _dot(hm, wo_ref[...])
    out_ref[0] = _tail(y, x_ref[0], p_ref[0], gpost_ref[...], wg_ref[...], wp_ref[...], gple_ref[...])


def _odd_out(h3, sg3, x3, p4, layer, cw, cb, lng, lnb, wo, gpost, wg, wp, gple):
    b, l, _ = h3.shape
    tc = min(CONV_TILE, l)
    ntile = l // tc
    cur = lambda bi, n: (bi, n, 0)
    prev, nxt = _halo_maps(tc // HALO, l // HALO)
    consts = (cw, cb, lng, lnb, wo, gpost, wg, wp, gple)
    return pl.pallas_call(
        functools.partial(_odd_out_kernel, tc=tc, ntile=ntile),
        grid=(b, ntile),
        in_specs=[pl.BlockSpec((1, tc, CONV_INNER), cur), pl.BlockSpec((1, HALO, CONV_INNER), prev),
                  pl.BlockSpec((1, HALO, CONV_INNER), nxt), pl.BlockSpec((1, tc, CONV_INNER), cur),
                  pl.BlockSpec((1, tc, D_MODEL), cur),
                  pl.BlockSpec((None, 1, tc, PLE_DIM), lambda bi, n: (layer, bi, n, 0))]
                 + [_const_spec(c.shape) for c in consts],
        out_specs=pl.BlockSpec((1, tc, D_MODEL), cur),
        out_shape=jax.ShapeDtypeStruct((b, l, D_MODEL), F32),
        scratch_shapes=[pltpu.VMEM((tc + 2 * HALO, CONV_INNER), F32),
                        pltpu.VMEM((SUBLANES, tc + 2 * HALO - SUBLANES, CONV_INNER), F32),
                        pltpu.VMEM((CONV_WIDTH, SUBLANES, CONV_INNER), F32),
                        pltpu.VMEM((tc, CONV_INNER), F32)],
        compiler_params=_params("parallel", "parallel"),
        name="odd_out",
    )(h3, h3, h3, sg3, x3, p4, *consts)


def _rope_tables(length):
    inv = 1.0 / (jnp.float32(ROPE_THETA) ** (jnp.arange(0, ROT_DIM, 2, dtype=F32) / ROT_DIM))
    ang = jnp.arange(length, dtype=F32)[:, None] * inv[None, :]
    cos, sin = jnp.cos(ang), jnp.sin(ang)
    m = jnp.arange(LANES) % HEAD_DIM
    idx = m % ROT_HALF
    cos_l = jnp.where(m < ROT_DIM, cos[:, idx], 1.0)
    sin_l = sin[:, idx]
    sa = jnp.where((m >= ROT_HALF) & (m < ROT_DIM), sin_l, 0.0)
    sb = jnp.where(m < ROT_HALF, -sin_l, 0.0)
    return cos_l, sa, sb, cos.T, sin.T


def _row(v):
    return v.reshape(1, -1).astype(F32)


def _pad_rows(w, rows):
    return jnp.concatenate([w, jnp.zeros((rows - w.shape[0],) + w.shape[1:], w.dtype)], axis=0)


def _pad_lanes(v, lanes=LANES):
    v = v.reshape(1, -1).astype(F32)
    return jnp.concatenate([v, jnp.zeros((1, lanes - v.shape[1]), F32)], axis=1)


def _even_weights(j, ev_w_in, ev_w_out, attn_sink, ssm_conv_w, ssm_conv_b, ssm_dt_bias, ssm_a_log, ssm_d,
                  ssm_norm):
    w = ev_w_in[j].astype(BF16)
    c0 = 0
    cols = []
    for width in (ATTN_Q, ATTN_KV, ATTN_KV, ATTN_Q, SSM_XBC, SSM_INNER, 2 * SSM_HEADS):
        cols.append(w[:, c0:c0 + width])
        c0 += width
    wq, wk, wv, wga, wxbc, wz, wdt = cols
    wdt = jnp.concatenate([wdt, jnp.zeros((D_MODEL, LANES - 2 * SSM_HEADS), BF16)], axis=1)
    wo = ev_w_out[j].astype(BF16)
    return dict(
        wqt=wq.T, wk=wk, wvt=wv.T, wga=wga, wxbc=wxbc, wz=wz, wdt=wdt, woa=wo[:ATTN_Q], wob=wo[ATTN_Q:],
        sink=attn_sink[j].astype(F32),
        conv_w=_pad_rows(ssm_conv_w[j].astype(F32), SUBLANES), conv_b=_row(ssm_conv_b[j]),
        dt_bias=_pad_lanes(ssm_dt_bias[j]), a_log=_pad_lanes(ssm_a_log[j]),
        dskip=_row(jnp.repeat(ssm_d[j].astype(F32), SSM_HEAD_DIM)), ssm_norm=_row(ssm_norm[j]))


def _odd_weights(j, od_w_in, od_conv_w, od_conv_b, od_ln_g, od_ln_b, od_w_out):
    w = od_w_in[j].astype(BF16)
    return dict(
        wa=w[:, :CONV_INNER], wb=w[:, CONV_INNER:2 * CONV_INNER], wg=w[:, 2 * CONV_INNER:],
        conv_w=_pad_rows(od_conv_w[j].astype(F32), 4 * SUBLANES), conv_b=_row(od_conv_b[j]),
        ln_g=_row(od_ln_g[j]), ln_b=_row(od_ln_b[j]), wo=od_w_out[j].astype(BF16))


def _trunk(x, p, layers, rope):
    b, l, _ = x.shape
    t = b * l
    flat = lambda a: a.reshape(t, a.shape[-1])
    seq = lambda a: a.reshape(b, l, a.shape[-1])
    p3 = p.reshape(p.shape[0], t, PLE_DIM)
    for i, lw in enumerate(layers):
        common = (lw["gpost"], lw["ple_wg"], lw["ple_wp"], lw["gple"])
        if i % 2 == 0:
            qt, k, vt, ga, xbcc, z, dt = _even_in(x, lw["gpre"], lw, rope)
            o_attn = _attention(lw["sink"], qt, k, vt, ga)
            y_f = _ssd(xbcc, dt, lw["dt_bias"], lw["a_log"], rev=False)
            o_ssm = _ssd(xbcc, dt, lw["dt_bias"], lw["a_log"], rev=True,
                         extra=(y_f, z, lw["dskip"], lw["ssm_norm"]))
            x = seq(_even_out(flat(o_attn), flat(o_ssm), flat(x), p3, i, lw["woa"], lw["wob"], *common, l))
        else:
            h, sg = _odd_in(flat(x), lw["gpre"], lw["wa"], lw["wb"], lw["wg"], l)
            x = _odd_out(seq(h), seq(sg), x, p, i, lw["conv_w"], lw["conv_b"], lw["ln_g"], lw["ln_b"], lw["wo"],
                         *common)
    return x


def kernel(x_prompt, x_sample, p_prompt, p_sample, norm_pre, norm_post, ple_w_gate, ple_w_proj, ple_norm, ev_w_in, ev_w_out, attn_sink, ssm_conv_w, ssm_conv_b, ssm_dt_bias, ssm_a_log, ssm_d, ssm_norm, od_w_in, od_conv_w, od_conv_b, od_ln_g, od_ln_b, od_w_out):
    layers = []
    for i in range(DEPTH):
        j = i // 2
        if i % 2 == 0:
            lw = _even_weights(j, ev_w_in, ev_w_out, attn_sink, ssm_conv_w, ssm_conv_b, ssm_dt_bias, ssm_a_log,
                               ssm_d, ssm_norm)
        else:
            lw = _odd_weights(j, od_w_in, od_conv_w, od_conv_b, od_ln_g, od_ln_b, od_w_out)
        lw.update(gpre=_row(norm_pre[i]), gpost=_row(norm_post[i]), ple_wg=ple_w_gate[i].astype(BF16),
                  ple_wp=ple_w_proj[i].astype(BF16), gple=_row(ple_norm[i]))
        layers.append(lw)
    y_prompt = _trunk(x_prompt, p_prompt, layers, _rope_tables(x_prompt.shape[1]))
    y_sample = _trunk(x_sample, p_sample, layers, _rope_tables(x_sample.shape[1]))
    return (y_prompt, y_sample)
```

```python
import functools

import jax
import jax.numpy as jnp
from jax import lax
from jax.experimental import pallas as pl
from jax.experimental.pallas import tpu as pltpu

F32 = jnp.float32
BF16 = jnp.bfloat16

D_MODEL = 1024
DEPTH = 4
PLE_DIM = 256
NORM_EPS = 1e-6
ATTN_HEADS = 16
ATTN_KV_HEADS = 4
HEAD_DIM = 64
ATTN_GROUP = ATTN_HEADS // ATTN_KV_HEADS
ATTN_BLOCK = 128
ROPE_THETA = 500000.0
ROT_DIM = HEAD_DIM // 4
ROT_HALF = ROT_DIM // 2
ATTN_Q = ATTN_HEADS * HEAD_DIM
ATTN_KV = ATTN_KV_HEADS * HEAD_DIM
SSM_INNER = D_MODEL
SSM_HEAD_DIM = 64
SSM_HEADS = SSM_INNER // SSM_HEAD_DIM
SSM_GROUPS = 2
SSM_STATE = 128
SSM_CONV = 5
SSM_CHUNK = 128
SSM_BC = 2 * SSM_GROUPS * SSM_STATE
SSM_XBC = SSM_INNER + SSM_BC
CONV_INNER = D_MODEL
CONV_WIDTH = 31

LANES = 128
SUBLANES = 8
HALO = 16
VMEM_LIMIT = 56 * 1024 * 1024

TOKEN_TILE = 512
WIDE_TILE = 1024
ATTN_TILE = 512
CONV_TILE = 256
SSD_STEP = 8
XBC_SLABS = 3
CONV_ROWS = 32
CONV_LANES = 512

NT_DIMS = (((1,), (1,)), ((), ()))
LOG2E = 1.4426950408889634


def _params(*sem):
    return pltpu.CompilerParams(dimension_semantics=sem, vmem_limit_bytes=VMEM_LIMIT)


def _const_spec(shape):
    nd = len(shape)
    return pl.BlockSpec(shape, lambda *_: (0,) * nd, pipeline_mode=pl.Buffered(1))


def _rms(x, g):
    ms = jnp.mean(x * x, axis=-1, keepdims=True)
    return x * lax.rsqrt(ms + NORM_EPS) * g


def _sigmoid(x):
    return jax.nn.sigmoid(x)


def _silu(x):
    return x * _sigmoid(x)


def _dot(a, b):
    return jnp.dot(a, b, preferred_element_type=F32)


def _dot_nt(a, b):
    return lax.dot_general(a, b, NT_DIMS, preferred_element_type=F32)


def _halo_maps(per, nh):
    prev = lambda bi, n: (bi, jnp.maximum(n * per - 1, 0), 0)
    nxt = lambda bi, n: (bi, jnp.minimum((n + 1) * per, nh - 1), 0)
    return prev, nxt


def _even_in_kernel(xc_ref, xp_ref, xn_ref, g_ref, wqt_ref, wk_ref, wvt_ref, wga_ref, wxbc_ref, wz_ref, wdt_ref,
                    cos_ref, sa_ref, sb_ref, cost_ref, sint_ref, cw_ref, cb_ref,
                    qt_ref, k_ref, vt_ref, ga_ref, xbc_ref, z_ref, dt_ref, *pad_refs, tm, ntile):
    n = pl.program_id(1)
    g = g_ref[...]
    u = _rms(xc_ref[0], g).astype(BF16)

    xh = jnp.concatenate([xp_ref[0], xn_ref[0]], axis=0)
    uh = _rms(xh, g).astype(BF16)
    sw = SSM_XBC // len(pad_refs)
    for c, pad_ref in enumerate(pad_refs):
        w = wxbc_ref[:, c * sw:(c + 1) * sw]
        hal = _dot(uh, w)
        pad_ref[0:SUBLANES] = jnp.where(n > 0, hal[0:SUBLANES], 0.0)
        pad_ref[SUBLANES:SUBLANES + tm] = _dot(u, w)
        pad_ref[SUBLANES + tm:] = jnp.where(n < ntile - 1, hal[SUBLANES:], 0.0)

    qt = _dot_nt(wqt_ref[...], u)
    ct = cost_ref[...]
    st = sint_ref[...]
    parts = []
    for h in range(ATTN_HEADS):
        b0 = h * HEAD_DIM
        r1 = qt[b0:b0 + ROT_HALF]
        r2 = qt[b0 + ROT_HALF:b0 + ROT_DIM]
        parts += [r1 * ct - r2 * st, r2 * ct + r1 * st, qt[b0 + ROT_DIM:b0 + HEAD_DIM]]
    scale = HEAD_DIM ** -0.5 * LOG2E
    qt = (jnp.concatenate(parts, axis=0) * scale).astype(BF16)
    vt = _dot_nt(wvt_ref[...], u).astype(BF16)
    for j in range(tm // ATTN_BLOCK):
        sl = slice(j * ATTN_BLOCK, (j + 1) * ATTN_BLOCK)
        qt_ref[0, j] = qt[:, sl]
        vt_ref[0, j] = vt[:, sl]

    cos = cos_ref[...]
    sa = sa_ref[...]
    sb = sb_ref[...]
    k = _dot(u, wk_ref[...])
    for j in range(ATTN_KV // LANES):
        sl = slice(j * LANES, (j + 1) * LANES)
        t = k[:, sl]
        k_ref[0, :, sl] = (t * cos + pltpu.roll(t, ROT_HALF, 1) * sa
                           + pltpu.roll(t, LANES - ROT_HALF, 1) * sb).astype(BF16)

    ga_ref[0] = _dot(u, wga_ref[...]).astype(BF16)
    z_ref[0] = _dot(u, wz_ref[...]).astype(BF16)
    dt_ref[0] = _dot(u, wdt_ref[...])

    half = SSM_CONV // 2
    for c, pad_ref in enumerate(pad_refs):
        cs = slice(c * sw, (c + 1) * sw)
        acc = jnp.zeros((tm, sw), F32) + cb_ref[:, cs]
        for kk in range(SSM_CONV):
            acc = acc + pad_ref[pl.ds(SUBLANES - half + kk, tm), :] * cw_ref[kk:kk + 1, cs]
        xbc_ref[0, :, cs] = _silu(acc).astype(BF16)


def _even_in(x3, g, lw, rope):
    b, l, _ = x3.shape
    tm = min(TOKEN_TILE, l)
    ntile = l // tm
    nb = tm // ATTN_BLOCK
    cos, sa, sb, cos_t, sin_t = rope
    cur = lambda bi, n: (bi, n, 0)
    prev, nxt = _halo_maps(tm // SUBLANES, l // SUBLANES)
    cur4 = lambda bi, n: (bi, n, 0, 0)
    pos = lambda bi, n: (n, 0)
    pos_t = lambda bi, n: (0, n)
    consts = (g, lw["wqt"], lw["wk"], lw["wvt"], lw["wga"], lw["wxbc"], lw["wz"], lw["wdt"])
    tail = (lw["conv_w"], lw["conv_b"])
    out_shape = [jax.ShapeDtypeStruct((b, l // ATTN_BLOCK, ATTN_Q, ATTN_BLOCK), BF16),
                 jax.ShapeDtypeStruct((b, l, ATTN_KV), BF16),
                 jax.ShapeDtypeStruct((b, l // ATTN_BLOCK, ATTN_KV, ATTN_BLOCK), BF16),
                 jax.ShapeDtypeStruct((b, l, ATTN_Q), BF16),
                 jax.ShapeDtypeStruct((b, l, SSM_XBC), BF16),
                 jax.ShapeDtypeStruct((b, l, SSM_INNER), BF16),
                 jax.ShapeDtypeStruct((b, l, LANES), F32)]
    out_specs = [pl.BlockSpec((1, nb, ATTN_Q, ATTN_BLOCK), cur4), pl.BlockSpec((1, tm, ATTN_KV), cur),
                 pl.BlockSpec((1, nb, ATTN_KV, ATTN_BLOCK), cur4), pl.BlockSpec((1, tm, ATTN_Q), cur),
                 pl.BlockSpec((1, tm, SSM_XBC), cur), pl.BlockSpec((1, tm, SSM_INNER), cur),
                 pl.BlockSpec((1, tm, LANES), cur)]
    return pl.pallas_call(
        functools.partial(_even_in_kernel, tm=tm, ntile=ntile),
        grid=(b, ntile),
        in_specs=[pl.BlockSpec((1, tm, D_MODEL), cur), pl.BlockSpec((1, SUBLANES, D_MODEL), prev),
                  pl.BlockSpec((1, SUBLANES, D_MODEL), nxt)]
                 + [_const_spec(c.shape) for c in consts]
                 + [pl.BlockSpec((tm, LANES), pos)] * 3 + [pl.BlockSpec((ROT_HALF, tm), pos_t)] * 2
                 + [_const_spec(c.shape) for c in tail],
        out_specs=out_specs,
        out_shape=out_shape,
        scratch_shapes=[pltpu.VMEM((tm + 2 * SUBLANES, SSM_XBC // XBC_SLABS), F32)] * XBC_SLABS,
        compiler_params=_params("parallel", "parallel"),
        name="even_in",
    )(x3, x3, x3, *consts, cos, sa, sb, cos_t, sin_t, *tail)


def _attn_kernel(sink_ref, qt_ref, kc_ref, kp_ref, kn_ref, vc_ref, vp_ref, vn_ref, ga_ref, o_ref,
                 kw_ref, vw_ref, *, tq, nblk):
    n = pl.program_id(1)
    blk = ATTN_BLOCK
    nb = tq // blk
    kw_ref[0:blk] = kp_ref[0]
    kw_ref[blk:blk + tq] = kc_ref[0]
    kw_ref[blk + tq:] = kn_ref[0]
    vw_ref[0] = vp_ref[0, 0]
    vw_ref[1:nb + 1] = vc_ref[0]
    vw_ref[nb + 1] = vn_ref[0, 0]
    kpos = lax.broadcasted_iota(jnp.int32, (blk, blk), 0)
    qpos = lax.broadcasted_iota(jnp.int32, (blk, blk), 1)
    lane = lax.broadcasted_iota(jnp.int32, (1, ATTN_GROUP * blk), 1)
    ones = jnp.ones((HALO, blk), BF16)
    neg = F32(-1e30)

    def body(j, carry):
        qb = n * nb + j
        vp = jnp.concatenate([(kpos >= qpos) & (qb > 0)] * ATTN_GROUP, axis=1)
        vn = jnp.concatenate([(kpos <= qpos) & (qb < nblk - 1)] * ATTN_GROUP, axis=1)
        row0 = pl.multiple_of(j * blk, blk)
        qt = qt_ref[0, j]
        kwin = kw_ref[pl.ds(row0, 3 * blk), :]
        vts = (vw_ref[j], vw_ref[j + 1], vw_ref[j + 2])

        def scores(g):
            q4 = jnp.concatenate(
                [qt[(g * ATTN_GROUP + hh) * HEAD_DIM:(g * ATTN_GROUP + hh + 1) * HEAD_DIM] for hh in range(ATTN_GROUP)],
                axis=1)
            kg = kwin[:, g * HEAD_DIM:(g + 1) * HEAD_DIM]
            return (jnp.where(vp, _dot(kg[0:blk], q4), neg),
                    _dot(kg[blk:2 * blk], q4),
                    jnp.where(vn, _dot(kg[2 * blk:], q4), neg))

        pend = [scores(g) for g in range(ATTN_KV_HEADS)]
        outs = []
        for g in range(ATTN_KV_HEADS):
            hd = slice(g * HEAD_DIM, (g + 1) * HEAD_DIM)
            sp, sc, sn = pend[g]
            sk = jnp.full((1, ATTN_GROUP * blk), sink_ref[g * ATTN_GROUP], F32)
            for hh in range(1, ATTN_GROUP):
                sk = jnp.where(lane >= hh * blk, sink_ref[g * ATTN_GROUP + hh], sk)
            sk = sk * LOG2E
            m = jnp.maximum(jnp.max(jnp.maximum(jnp.maximum(sp, sc), sn), axis=0, keepdims=True), sk)
            ot = None
            for v, sb in zip(vts, (sp, sc, sn)):
                ve = jnp.concatenate([v[hd], ones], axis=0)
                part = _dot(ve, jnp.exp2(sb - m).astype(BF16))
                ot = part if ot is None else ot + part
            den = ot[HEAD_DIM:HEAD_DIM + 1] + jnp.exp2(sk - m)
            ot = ot[0:HEAD_DIM] * (1.0 / den)
            ot = jnp.concatenate([ot[:, hh * blk:(hh + 1) * blk] for hh in range(ATTN_GROUP)], axis=0)
            outs.append(ot.T)
        o_all = jnp.concatenate(outs, axis=1)
        ga = ga_ref[0, pl.ds(row0, blk), :].astype(F32)
        o_ref[0, pl.ds(row0, blk), :] = (o_all * _silu(ga)).astype(BF16)
        return carry

    lax.fori_loop(0, nb, body, 0)


def _attention(sink, qt4, k3, vt4, ga3):
    b, l, _ = k3.shape
    tq = min(ATTN_TILE, l)
    per = tq // ATTN_BLOCK
    nblk = l // ATTN_BLOCK
    cur = lambda bi, n: (bi, n, 0)
    prev, nxt = _halo_maps(per, nblk)
    cur4 = lambda bi, n: (bi, n, 0, 0)
    prev4 = lambda bi, n: prev(bi, n) + (0,)
    nxt4 = lambda bi, n: nxt(bi, n) + (0,)
    return pl.pallas_call(
        functools.partial(_attn_kernel, tq=tq, nblk=nblk),
        grid=(b, l // tq),
        in_specs=[pl.BlockSpec(memory_space=pltpu.SMEM),
                  pl.BlockSpec((1, per, ATTN_Q, ATTN_BLOCK), cur4),
                  pl.BlockSpec((1, tq, ATTN_KV), cur), pl.BlockSpec((1, ATTN_BLOCK, ATTN_KV), prev),
                  pl.BlockSpec((1, ATTN_BLOCK, ATTN_KV), nxt),
                  pl.BlockSpec((1, per, ATTN_KV, ATTN_BLOCK), cur4), pl.BlockSpec((1, 1, ATTN_KV, ATTN_BLOCK), prev4),
                  pl.BlockSpec((1, 1, ATTN_KV, ATTN_BLOCK), nxt4),
                  pl.BlockSpec((1, tq, ATTN_Q), cur)],
        out_specs=pl.BlockSpec((1, tq, ATTN_Q), cur),
        out_shape=jax.ShapeDtypeStruct((b, l, ATTN_Q), BF16),
        scratch_shapes=[pltpu.VMEM((tq + 2 * ATTN_BLOCK, ATTN_KV), BF16),
                        pltpu.VMEM((per + 2, ATTN_KV, ATTN_BLOCK), BF16)],
        compiler_params=_params("parallel", "parallel"),
        name="attention",
    )(sink, qt4, k3, k3, k3, vt4, vt4, vt4, ga3)


def _ssd_decay(dtr, bias, alog, rev):
    ch = SSM_CHUNK
    dt = jax.nn.softplus(dtr + bias)
    a = -jnp.exp(alog) * LOG2E
    v = dt * a
    ri = lax.broadcasted_iota(jnp.int32, (ch, ch), 0)
    ci = lax.broadcasted_iota(jnp.int32, (ch, ch), 1)
    tri = (ci >= ri) if rev else (ci <= ri)
    trib = jnp.where(tri, 1.0, 0.0).astype(BF16)
    v1 = v.astype(BF16)
    r1 = v - v1.astype(F32)
    v2 = r1.astype(BF16)
    v3 = (r1 - v2.astype(F32)).astype(BF16)
    acs = _dot(trib, v1) + _dot(trib, v2) + _dot(trib, v3)
    tot = acs[0:1, :] if rev else acs[ch - 1:ch, :]
    te = jnp.exp2(tot - acs) * dt
    cdec = jnp.exp2(tot)
    acs_t = acs.T
    arow_t = acs_t - jnp.log(dt.T) * LOG2E
    return acs, arow_t, te.T, cdec, tri


def _ssd_local(xbc, decay, rev):
    ch = SSM_CHUNK
    hp = SSM_HEADS // SSM_GROUPS
    gw = hp * SSM_HEAD_DIM
    off = SSM_HEADS if rev else 0
    acs, arow_t, te_t, cdec, tri = decay
    lo = lax.broadcasted_iota(jnp.int32, (ch, LANES), 1) < SSM_HEAD_DIM
    lo1 = lo[0:1, :]

    xs = xbc[:, 0:SSM_INNER]
    groups = []
    for g in range(SSM_GROUPS):
        bg = xbc[:, SSM_INNER + g * SSM_STATE:SSM_INNER + (g + 1) * SSM_STATE]
        cg = xbc[:, SSM_INNER + (SSM_GROUPS + g) * SSM_STATE:SSM_INNER + (SSM_GROUPS + g + 1) * SSM_STATE]
        cb = _dot_nt(cg, bg)
        bt = bg.astype(F32).T
        eacs_x, cdec_x, yd, sn = [], [], [], []
        for k in range(hp // 2):
            h0 = off + g * hp + 2 * k
            xp = xs[:, g * gw + k * LANES:g * gw + (k + 1) * LANES]
            ms, bs, ea = [], [], []
            for e in range(2):
                h = h0 + e
                acol = jnp.broadcast_to(acs[:, h:h + 1], (ch, ch))
                dec = jnp.exp2(jnp.where(tri, acol - arow_t[h:h + 1, :], -jnp.inf))
                ms.append((cb * dec).astype(BF16))
                bs.append((bt * te_t[h:h + 1, :]).astype(BF16))
                ea.append(jnp.exp2(acol))
            zero = jnp.zeros_like(xp)
            rhs = jnp.concatenate([jnp.where(lo, xp, zero), jnp.where(lo, zero, xp)], axis=0)
            yd.append(_dot(jnp.concatenate(ms, axis=1), rhs))
            sn.append(_dot(jnp.concatenate(bs, axis=1), rhs))
            eacs_x.append(jnp.where(lo, ea[0], ea[1]))
            cdec_x.append(jnp.where(lo1, cdec[:, h0:h0 + 1], cdec[:, h0 + 1:h0 + 2]))
        groups.append((cg, jnp.concatenate(yd, axis=1), jnp.concatenate(sn, axis=1),
                       jnp.concatenate(eacs_x, axis=1), jnp.concatenate(cdec_x, axis=1)))
    return xs, groups


def _ssd_kernel(*refs, rev, nsub):
    if rev:
        (xbc_ref, dt_ref, bias_ref, alog_ref, yf_ref, z_ref, dskip_ref, gn_ref, out_ref, st_ref) = refs
    else:
        (xbc_ref, dt_ref, bias_ref, alog_ref, out_ref, st_ref) = refs
    ch = SSM_CHUNK

    @pl.when(pl.program_id(1) == 0)
    def _():
        st_ref[...] = jnp.zeros_like(st_ref)

    order = list(range(nsub - 1, -1, -1) if rev else range(nsub))
    rows = [slice(sc * ch, (sc + 1) * ch) for sc in range(nsub)]
    decay = {sc: _ssd_decay(dt_ref[0, rows[sc], :], bias_ref[...], alog_ref[...], rev) for sc in order}
    for sc in order:
        xs, groups = _ssd_local(xbc_ref[0, rows[sc], :], decay[sc], rev)
        ys = []
        for g, (cg, yd, sn, eacs_x, cdec_x) in enumerate(groups):
            st_prev = st_ref[g]
            ys.append(yd + _dot(cg, st_prev.astype(BF16)) * eacs_x)
            st_ref[g] = st_prev * cdec_x + sn
        y = jnp.concatenate(ys, axis=1)
        if rev:
            y = y + yf_ref[0, rows[sc], :] + dskip_ref[...] * xs.astype(F32)
            y = y * _silu(z_ref[0, rows[sc], :].astype(F32))
            out_ref[0, rows[sc], :] = _rms(y, gn_ref[...]).astype(BF16)
        else:
            out_ref[0, rows[sc], :] = y


def _ssd(xbcc3, dt3, bias, alog, rev, extra=()):
    b, l, _ = xbcc3.shape
    nsub = min(SSD_STEP, l // SSM_CHUNK)
    nc = l // (nsub * SSM_CHUNK)
    idx = (lambda bi, c: (bi, nc - 1 - c, 0)) if rev else (lambda bi, c: (bi, c, 0))
    blk = lambda w: pl.BlockSpec((1, nsub * SSM_CHUNK, w), idx)
    in_specs = [blk(SSM_XBC), blk(LANES), _const_spec(bias.shape), _const_spec(alog.shape)]
    if rev:
        yf, z3, dskip, gn = extra
        in_specs += [blk(SSM_INNER), blk(SSM_INNER), _const_spec(dskip.shape), _const_spec(gn.shape)]
    return pl.pallas_call(
        functools.partial(_ssd_kernel, rev=rev, nsub=nsub),
        grid=(b, nc),
        in_specs=in_specs,
        out_specs=blk(SSM_INNER),
        out_shape=jax.ShapeDtypeStruct((b, l, SSM_INNER), BF16 if rev else F32),
        scratch_shapes=[pltpu.VMEM((SSM_GROUPS, SSM_STATE, SSM_INNER // SSM_GROUPS), F32)],
        compiler_params=_params("parallel", "arbitrary"),
        name="ssd_bwd" if rev else "ssd_fwd",
    )(xbcc3, dt3, bias, alog, *extra)


def _tail(y, x, p, gpost, wg, wp, gple):
    x1 = x + _rms(y, gpost)
    gate = _sigmoid(_dot(x1.astype(BF16), wg))
    pp = _dot(p.astype(BF16), wp)
    return x1 + _rms(gate * pp, gple)


def _even_out_kernel(oa_ref, ob_ref, x_ref, p_ref, woa_ref, wob_ref, gpost_ref, wg_ref, wp_ref, gple_ref,
                     out_ref):
    y = _dot(oa_ref[...], woa_ref[...]) + _dot(ob_ref[...], wob_ref[...])
    out_ref[...] = _tail(y, x_ref[...], p_ref[...], gpost_ref[...], wg_ref[...], wp_ref[...], gple_ref[...])


def _even_out(oa, ob, x2, p3, layer, woa, wob, gpost, wg, wp, gple, seq_len):
    t = x2.shape[0]
    tm = min(WIDE_TILE, seq_len)
    row = lambda n: (n, 0)
    consts = (woa, wob, gpost, wg, wp, gple)
    return pl.pallas_call(
        _even_out_kernel,
        grid=(t // tm,),
        in_specs=[pl.BlockSpec((tm, ATTN_Q), row), pl.BlockSpec((tm, SSM_INNER), row),
                  pl.BlockSpec((tm, D_MODEL), row), pl.BlockSpec((None, tm, PLE_DIM), lambda n: (layer, n, 0))]
                 + [_const_spec(c.shape) for c in consts],
        out_specs=pl.BlockSpec((tm, D_MODEL), row),
        out_shape=jax.ShapeDtypeStruct((t, D_MODEL), F32),
        compiler_params=_params("parallel"),
        name="even_out",
    )(oa, ob, x2, p3, *consts)


def _odd_in_kernel(x_ref, g_ref, wa_ref, wb_ref, wg_ref, h_ref, sg_ref):
    u = _rms(x_ref[...], g_ref[...]).astype(BF16)
    a = _dot(u, wa_ref[...])
    b = _dot(u, wb_ref[...])
    h_ref[...] = (a * _sigmoid(b)).astype(BF16)
    sg_ref[...] = _silu(_dot(u, wg_ref[...])).astype(BF16)


def _odd_in(x2, g, wa, wb, wg, seq_len):
    t = x2.shape[0]
    tm = min(WIDE_TILE, seq_len)
    row = lambda n: (n, 0)
    return pl.pallas_call(
        _odd_in_kernel,
        grid=(t // tm,),
        in_specs=[pl.BlockSpec((tm, D_MODEL), row), _const_spec(g.shape), _const_spec(wa.shape),
                  _const_spec(wb.shape), _const_spec(wg.shape)],
        out_specs=[pl.BlockSpec((tm, CONV_INNER), row), pl.BlockSpec((tm, CONV_INNER), row)],
        out_shape=[jax.ShapeDtypeStruct((t, CONV_INNER), BF16), jax.ShapeDtypeStruct((t, CONV_INNER), BF16)],
        compiler_params=_params("parallel"),
        name="odd_in",
    )(x2, g, wa, wb, wg)


def _odd_out_kernel(hc_ref, hp_ref, hn_ref, sg_ref, x_ref, p_ref, cw_ref, cb_ref, lng_ref, lnb_ref, wo_ref,
                    gpost_ref, wg_ref, wp_ref, gple_ref, out_ref, pad_ref, sh_ref, wb_ref, cv_ref, *, tc, ntile):
    n = pl.program_id(1)
    pad_ref[0:HALO] = jnp.where(n > 0, hp_ref[0].astype(F32), 0.0)
    pad_ref[HALO:HALO + tc] = hc_ref[0].astype(F32)
    pad_ref[HALO + tc:] = jnp.where(n < ntile - 1, hn_ref[0].astype(F32), 0.0)
    half = CONV_WIDTH // 2
    rows = CONV_ROWS
    span = sh_ref.shape[1]
    for c in range(SUBLANES):
        sh_ref[c] = pad_ref[pl.ds(c, span), :]
    for k in range(CONV_WIDTH):
        wb_ref[k] = jnp.broadcast_to(cw_ref[k:k + 1, :], (SUBLANES, CONV_INNER))

    def body(i, carry):
        r0 = pl.multiple_of(i * rows, rows)
        for l0 in range(0, CONV_INNER, CONV_LANES):
            ls = slice(l0, l0 + CONV_LANES)
            accs = [jnp.zeros((SUBLANES, CONV_LANES), F32) + cb_ref[:, ls] for _ in range(rows // SUBLANES)]
            for k in range(CONV_WIDTH):
                o = HALO - half + k
                w = wb_ref[k, :, ls]
                for r in range(rows // SUBLANES):
                    a0 = r0 + o - o % SUBLANES + r * SUBLANES
                    accs[r] = accs[r] + sh_ref[o % SUBLANES, pl.ds(a0, SUBLANES), ls] * w
            for r in range(rows // SUBLANES):
                cv_ref[pl.ds(r0 + r * SUBLANES, SUBLANES), ls] = accs[r]
        return carry

    lax.fori_loop(0, tc // rows, body, 0)
    cv = cv_ref[...]
    mu = jnp.mean(cv, axis=-1, keepdims=True)
    cen = cv - mu
    var = jnp.mean(cen * cen, axis=-1, keepdims=True)
    hn = _silu(cen * lax.rsqrt(var + NORM_EPS) * lng_ref[...] + lnb_ref[...])
    hm = (hn * sg_ref[0].astype(F32)).astype(BF16)
    y = _dot(hm, wo_ref[...])
    out_ref[0] = _tail(y, x_ref[0], p_ref[0], gpost_ref[...], wg_ref[...], wp_ref[...], gple_ref[...])


def _odd_out(h3, sg3, x3, p4, layer, cw, cb, lng, lnb, wo, gpost, wg, wp, gple):
    b, l, _ = h3.shape
    tc = min(CONV_TILE, l)
    ntile = l // tc
    cur = lambda bi, n: (bi, n, 0)
    prev, nxt = _halo_maps(tc // HALO, l // HALO)
    consts = (cw, cb, lng, lnb, wo, gpost, wg, wp, gple)
    return pl.pallas_call(
        functools.partial(_odd_out_kernel, tc=tc, ntile=ntile),
        grid=(b, ntile),
        in_specs=[pl.BlockSpec((1, tc, CONV_INNER), cur), pl.BlockSpec((1, HALO, CONV_INNER), prev),
                  pl.BlockSpec((1, HALO, CONV_INNER), nxt), pl.BlockSpec((1, tc, CONV_INNER), cur),
                  pl.BlockSpec((1, tc, D_MODEL), cur),
                  pl.BlockSpec((None, 1, tc, PLE_DIM), lambda bi, n: (layer, bi, n, 0))]
                 + [_const_spec(c.shape) for c in consts],
        out_specs=pl.BlockSpec((1, tc, D_MODEL), cur),
        out_shape=jax.ShapeDtypeStruct((b, l, D_MODEL), F32),
        scratch_shapes=[pltpu.VMEM((tc + 2 * HALO, CONV_INNER), F32),
                        pltpu.VMEM((SUBLANES, tc + 2 * HALO - SUBLANES, CONV_INNER), F32),
                        pltpu.VMEM((CONV_WIDTH, SUBLANES, CONV_INNER), F32),
                        pltpu.VMEM((tc, CONV_INNER), F32)],
        compiler_params=_params("parallel", "parallel"),
        name="odd_out",
    )(h3, h3, h3, sg3, x3, p4, *consts)


def _rope_tables(length):
    inv = 1.0 / (jnp.float32(ROPE_THETA) ** (jnp.arange(0, ROT_DIM, 2, dtype=F32) / ROT_DIM))
    ang = jnp.arange(length, dtype=F32)[:, None] * inv[None, :]
    cos, sin = jnp.cos(ang), jnp.sin(ang)
    m = jnp.arange(LANES) % HEAD_DIM
    idx = m % ROT_HALF
    cos_l = jnp.where(m < ROT_DIM, cos[:, idx], 1.0)
    sin_l = sin[:, idx]
    sa = jnp.where((m >= ROT_HALF) & (m < ROT_DIM), sin_l, 0.0)
    sb = jnp.where(m < ROT_HALF, -sin_l, 0.0)
    return cos_l, sa, sb, cos.T, sin.T


def _row(v):
    return v.reshape(1, -1).astype(F32)


def _pad_rows(w, rows):
    return jnp.concatenate([w, jnp.zeros((rows - w.shape[0],) + w.shape[1:], w.dtype)], axis=0)


def _pad_lanes(v, lanes=LANES):
    v = v.reshape(1, -1).astype(F32)
    return jnp.concatenate([v, jnp.zeros((1, lanes - v.shape[1]), F32)], axis=1)


def _even_weights(j, ev_w_in, ev_w_out, attn_sink, ssm_conv_w, ssm_conv_b, ssm_dt_bias, ssm_a_log, ssm_d,
                  ssm_norm):
    w = ev_w_in[j].astype(BF16)
    c0 = 0
    cols = []
    for width in (ATTN_Q, ATTN_KV, ATTN_KV, ATTN_Q, SSM_XBC, SSM_INNER, 2 * SSM_HEADS):
        cols.append(w[:, c0:c0 + width])
        c0 += width
    wq, wk, wv, wga, wxbc, wz, wdt = cols
    wdt = jnp.concatenate([wdt, jnp.zeros((D_MODEL, LANES - 2 * SSM_HEADS), BF16)], axis=1)
    wo = ev_w_out[j].astype(BF16)
    return dict(
        wqt=wq.T, wk=wk, wvt=wv.T, wga=wga, wxbc=wxbc, wz=wz, wdt=wdt, woa=wo[:ATTN_Q], wob=wo[ATTN_Q:],
        sink=attn_sink[j].astype(F32),
        conv_w=_pad_rows(ssm_conv_w[j].astype(F32), SUBLANES), conv_b=_row(ssm_conv_b[j]),
        dt_bias=_pad_lanes(ssm_dt_bias[j]), a_log=_pad_lanes(ssm_a_log[j]),
        dskip=_row(jnp.repeat(ssm_d[j].astype(F32), SSM_HEAD_DIM)), ssm_norm=_row(ssm_norm[j]))


def _odd_weights(j, od_w_in, od_conv_w, od_conv_b, od_ln_g, od_ln_b, od_w_out):
    w = od_w_in[j].astype(BF16)
    return dict(
        wa=w[:, :CONV_INNER], wb=w[:, CONV_INNER:2 * CONV_INNER], wg=w[:, 2 * CONV_INNER:],
        conv_w=_pad_rows(od_conv_w[j].astype(F32), 4 * SUBLANES), conv_b=_row(od_conv_b[j]),
        ln_g=_row(od_ln_g[j]), ln_b=_row(od_ln_b[j]), wo=od_w_out[j].astype(BF16))


def _trunk(x, p, layers, rope):
    b, l, _ = x.shape
    t = b * l
    flat = lambda a: a.reshape(t, a.shape[-1])
    seq = lambda a: a.reshape(b, l, a.shape[-1])
    p3 = p.reshape(p.shape[0], t, PLE_DIM)
    for i, lw in enumerate(layers):
        common = (lw["gpost"], lw["ple_wg"], lw["ple_wp"], lw["gple"])
        if i % 2 == 0:
            qt, k, vt, ga, xbcc, z, dt = _even_in(x, lw["gpre"], lw, rope)
            o_attn = _attention(lw["sink"], qt, k, vt, ga)
            y_f = _ssd(xbcc, dt, lw["dt_bias"], lw["a_log"], rev=False)
            o_ssm = _ssd(xbcc, dt, lw["dt_bias"], lw["a_log"], rev=True,
                         extra=(y_f, z, lw["dskip"], lw["ssm_norm"]))
            x = seq(_even_out(flat(o_attn), flat(o_ssm), flat(x), p3, i, lw["woa"], lw["wob"], *common, l))
        else:
            h, sg = _odd_in(flat(x), lw["gpre"], lw["wa"], lw["wb"], lw["wg"], l)
            x = _odd_out(seq(h), seq(sg), x, p, i, lw["conv_w"], lw["conv_b"], lw["ln_g"], lw["ln_b"], lw["wo"],
                         *common)
    return x


def kernel(x_prompt, x_sample, p_prompt, p_sample, norm_pre, norm_post, ple_w_gate, ple_w_proj, ple_norm, ev_w_in, ev_w_out, attn_sink, ssm_conv_w, ssm_conv_b, ssm_dt_bias, ssm_a_log, ssm_d, ssm_norm, od_w_in, od_conv_w, od_conv_b, od_ln_g, od_ln_b, od_w_out):
    layers = []
    for i in range(DEPTH):
        j = i // 2
        if i % 2 == 0:
            lw = _even_weights(j, ev_w_in, ev_w_out, attn_sink, ssm_conv_w, ssm_conv_b, ssm_dt_bias, ssm_a_log,
                               ssm_d, ssm_norm)
        else:
            lw = _odd_weights(j, od_w_in, od_conv_w, od_conv_b, od_ln_g, od_ln_b, od_w_out)
        lw.update(gpre=_row(norm_pre[i]), gpost=_row(norm_post[i]), ple_wg=ple_w_gate[i].astype(BF16),
                  ple_wp=ple_w_proj[i].astype(BF16), gple=_row(ple_norm[i]))
        layers.append(lw)
    y_prompt = _trunk(x_prompt, p_prompt, layers, _rope_tables(x_prompt.shape[1]))
    y_sample = _trunk(x_sample, p_sample, layers, _rope_tables(x_sample.shape[1]))
    return (y_prompt, y_sample)
```

```python
import functools

import jax
import jax.numpy as jnp
from jax import lax
from jax.experimental import pallas as pl
from jax.experimental.pallas import tpu as pltpu

F32 = jnp.float32
BF16 = jnp.bfloat16

D_MODEL = 1024
DEPTH = 4
PLE_DIM = 256
NORM_EPS = 1e-6
ATTN_HEADS = 16
ATTN_KV_HEADS = 4
HEAD_DIM = 64
ATTN_GROUP = ATTN_HEADS // ATTN_KV_HEADS
ATTN_BLOCK = 128
ROPE_THETA = 500000.0
ROT_DIM = HEAD_DIM // 4
ROT_HALF = ROT_DIM // 2
ATTN_Q = ATTN_HEADS * HEAD_DIM
ATTN_KV = ATTN_KV_HEADS * HEAD_DIM
SSM_INNER = D_MODEL
SSM_HEAD_DIM = 64
SSM_HEADS = SSM_INNER // SSM_HEAD_DIM
SSM_GROUPS = 2
SSM_STATE = 128
SSM_CONV = 5
SSM_CHUNK = 128
SSM_BC = 2 * SSM_GROUPS * SSM_STATE
SSM_XBC = SSM_INNER + SSM_BC
CONV_INNER = D_MODEL
CONV_WIDTH = 31

LANES = 128
SUBLANES = 8
HALO = 16
VMEM_LIMIT = 56 * 1024 * 1024

TOKEN_TILE = 512
WIDE_TILE = 1024
ATTN_TILE = 512
CONV_TILE = 512
SSD_STEP = 8
XBC_SLABS = 3
CONV_ROWS = 32
CONV_LANES = 512

NT_DIMS = (((1,), (1,)), ((), ()))
LOG2E = 1.4426950408889634


def _params(*sem):
    return pltpu.CompilerParams(dimension_semantics=sem, vmem_limit_bytes=VMEM_LIMIT)


def _const_spec(shape):
    nd = len(shape)
    return pl.BlockSpec(shape, lambda *_: (0,) * nd, pipeline_mode=pl.Buffered(1))


def _rms(x, g):
    ms = jnp.mean(x * x, axis=-1, keepdims=True)
    return x * lax.rsqrt(ms + NORM_EPS) * g


def _sigmoid(x):
    return jax.nn.sigmoid(x)


def _silu(x):
    return x * _sigmoid(x)


def _dot(a, b):
    return jnp.dot(a, b, preferred_element_type=F32)


def _dot_nt(a, b):
    return lax.dot_general(a, b, NT_DIMS, preferred_element_type=F32)


def _halo_maps(per, nh):
    prev = lambda bi, n: (bi, jnp.maximum(n * per - 1, 0), 0)
    nxt = lambda bi, n: (bi, jnp.minimum((n + 1) * per, nh - 1), 0)
    return prev, nxt


def _even_in_kernel(xc_ref, xp_ref, xn_ref, g_ref, wqt_ref, wk_ref, wvt_ref, wga_ref, wxbc_ref, wz_ref, wdt_ref,
                    cos_ref, sa_ref, sb_ref, cost_ref, sint_ref, cw_ref, cb_ref,
                    qt_ref, k_ref, vt_ref, ga_ref, xbc_ref, z_ref, dt_ref, *pad_refs, tm, ntile):
    n = pl.program_id(1)
    g = g_ref[...]
    u = _rms(xc_ref[0], g).astype(BF16)

    xh = jnp.concatenate([xp_ref[0], xn_ref[0]], axis=0)
    uh = _rms(xh, g).astype(BF16)
    sw = SSM_XBC // len(pad_refs)
    for c, pad_ref in enumerate(pad_refs):
        w = wxbc_ref[:, c * sw:(c + 1) * sw]
        hal = _dot(uh, w)
        pad_ref[0:SUBLANES] = jnp.where(n > 0, hal[0:SUBLANES], 0.0)
        pad_ref[SUBLANES:SUBLANES + tm] = _dot(u, w)
        pad_ref[SUBLANES + tm:] = jnp.where(n < ntile - 1, hal[SUBLANES:], 0.0)

    qt = _dot_nt(wqt_ref[...], u)
    ct = cost_ref[...]
    st = sint_ref[...]
    parts = []
    for h in range(ATTN_HEADS):
        b0 = h * HEAD_DIM
        r1 = qt[b0:b0 + ROT_HALF]
        r2 = qt[b0 + ROT_HALF:b0 + ROT_DIM]
        parts += [r1 * ct - r2 * st, r2 * ct + r1 * st, qt[b0 + ROT_DIM:b0 + HEAD_DIM]]
    scale = HEAD_DIM ** -0.5 * LOG2E
    qt = (jnp.concatenate(parts, axis=0) * scale).astype(BF16)
    vt = _dot_nt(wvt_ref[...], u).astype(BF16)
    for j in range(tm // ATTN_BLOCK):
        sl = slice(j * ATTN_BLOCK, (j + 1) * ATTN_BLOCK)
        qt_ref[0, j] = qt[:, sl]
        vt_ref[0, j] = vt[:, sl]

    cos = cos_ref[...]
    sa = sa_ref[...]
    sb = sb_ref[...]
    k = _dot(u, wk_ref[...])
    for j in range(ATTN_KV // LANES):
        sl = slice(j * LANES, (j + 1) * LANES)
        t = k[:, sl]
        k_ref[0, :, sl] = (t * cos + pltpu.roll(t, ROT_HALF, 1) * sa
                           + pltpu.roll(t, LANES - ROT_HALF, 1) * sb).astype(BF16)

    ga_ref[0] = _dot(u, wga_ref[...]).astype(BF16)
    z_ref[0] = _dot(u, wz_ref[...]).astype(BF16)
    dt_ref[0] = _dot(u, wdt_ref[...])

    half = SSM_CONV // 2
    for c, pad_ref in enumerate(pad_refs):
        cs = slice(c * sw, (c + 1) * sw)
        acc = jnp.zeros((tm, sw), F32) + cb_ref[:, cs]
        for kk in range(SSM_CONV):
            acc = acc + pad_ref[pl.ds(SUBLANES - half + kk, tm), :] * cw_ref[kk:kk + 1, cs]
        xbc_ref[0, :, cs] = _silu(acc).astype(BF16)


def _even_in(x3, g, lw, rope):
    b, l, _ = x3.shape
    tm = min(TOKEN_TILE, l)
    ntile = l // tm
    nb = tm // ATTN_BLOCK
    cos, sa, sb, cos_t, sin_t = rope
    cur = lambda bi, n: (bi, n, 0)
    prev, nxt = _halo_maps(tm // SUBLANES, l // SUBLANES)
    cur4 = lambda bi, n: (bi, n, 0, 0)
    pos = lambda bi, n: (n, 0)
    pos_t = lambda bi, n: (0, n)
    consts = (g, lw["wqt"], lw["wk"], lw["wvt"], lw["wga"], lw["wxbc"], lw["wz"], lw["wdt"])
    tail = (lw["conv_w"], lw["conv_b"])
    out_shape = [jax.ShapeDtypeStruct((b, l // ATTN_BLOCK, ATTN_Q, ATTN_BLOCK), BF16),
                 jax.ShapeDtypeStruct((b, l, ATTN_KV), BF16),
                 jax.ShapeDtypeStruct((b, l // ATTN_BLOCK, ATTN_KV, ATTN_BLOCK), BF16),
                 jax.ShapeDtypeStruct((b, l, ATTN_Q), BF16),
                 jax.ShapeDtypeStruct((b, l, SSM_XBC), BF16),
                 jax.ShapeDtypeStruct((b, l, SSM_INNER), BF16),
                 jax.ShapeDtypeStruct((b, l, LANES), F32)]
    out_specs = [pl.BlockSpec((1, nb, ATTN_Q, ATTN_BLOCK), cur4), pl.BlockSpec((1, tm, ATTN_KV), cur),
                 pl.BlockSpec((1, nb, ATTN_KV, ATTN_BLOCK), cur4), pl.BlockSpec((1, tm, ATTN_Q), cur),
                 pl.BlockSpec((1, tm, SSM_XBC), cur), pl.BlockSpec((1, tm, SSM_INNER), cur),
                 pl.BlockSpec((1, tm, LANES), cur)]
    return pl.pallas_call(
        functools.partial(_even_in_kernel, tm=tm, ntile=ntile),
        grid=(b, ntile),
        in_specs=[pl.BlockSpec((1, tm, D_MODEL), cur), pl.BlockSpec((1, SUBLANES, D_MODEL), prev),
                  pl.BlockSpec((1, SUBLANES, D_MODEL), nxt)]
                 + [_const_spec(c.shape) for c in consts]
                 + [pl.BlockSpec((tm, LANES), pos)] * 3 + [pl.BlockSpec((ROT_HALF, tm), pos_t)] * 2
                 + [_const_spec(c.shape) for c in tail],
        out_specs=out_specs,
        out_shape=out_shape,
        scratch_shapes=[pltpu.VMEM((tm + 2 * SUBLANES, SSM_XBC // XBC_SLABS), F32)] * XBC_SLABS,
        compiler_params=_params("parallel", "parallel"),
        name="even_in",
    )(x3, x3, x3, *consts, cos, sa, sb, cos_t, sin_t, *tail)


def _attn_kernel(sink_ref, qt_ref, kc_ref, kp_ref, kn_ref, vc_ref, vp_ref, vn_ref, ga_ref, o_ref,
                 kw_ref, vw_ref, *, tq, nblk):
    n = pl.program_id(1)
    blk = ATTN_BLOCK
    nb = tq // blk
    kw_ref[0:blk] = kp_ref[0]
    kw_ref[blk:blk + tq] = kc_ref[0]
    kw_ref[blk + tq:] = kn_ref[0]
    vw_ref[0] = vp_ref[0, 0]
    vw_ref[1:nb + 1] = vc_ref[0]
    vw_ref[nb + 1] = vn_ref[0, 0]
    kpos = lax.broadcasted_iota(jnp.int32, (blk, blk), 0)
    qpos = lax.broadcasted_iota(jnp.int32, (blk, blk), 1)
    lane = lax.broadcasted_iota(jnp.int32, (1, ATTN_GROUP * blk), 1)
    ones = jnp.ones((HALO, blk), BF16)
    neg = F32(-1e30)

    def body(j, carry):
        qb = n * nb + j
        vp = jnp.concatenate([(kpos >= qpos) & (qb > 0)] * ATTN_GROUP, axis=1)
        vn = jnp.concatenate([(kpos <= qpos) & (qb < nblk - 1)] * ATTN_GROUP, axis=1)
        row0 = pl.multiple_of(j * blk, blk)
        qt = qt_ref[0, j]
        kwin = kw_ref[pl.ds(row0, 3 * blk), :]
        vts = (vw_ref[j], vw_ref[j + 1], vw_ref[j + 2])

        def scores(g):
            q4 = jnp.concatenate(
                [qt[(g * ATTN_GROUP + hh) * HEAD_DIM:(g * ATTN_GROUP + hh + 1) * HEAD_DIM] for hh in range(ATTN_GROUP)],
                axis=1)
            kg = kwin[:, g * HEAD_DIM:(g + 1) * HEAD_DIM]
            return (jnp.where(vp, _dot(kg[0:blk], q4), neg),
                    _dot(kg[blk:2 * blk], q4),
                    jnp.where(vn, _dot(kg[2 * blk:], q4), neg))

        pend = [scores(g) for g in range(ATTN_KV_HEADS)]
        outs = []
        for g in range(ATTN_KV_HEADS):
            hd = slice(g * HEAD_DIM, (g + 1) * HEAD_DIM)
            sp, sc, sn = pend[g]
            sk = jnp.full((1, ATTN_GROUP * blk), sink_ref[g * ATTN_GROUP], F32)
            for hh in range(1, ATTN_GROUP):
                sk = jnp.where(lane >= hh * blk, sink_ref[g * ATTN_GROUP + hh], sk)
            sk = sk * LOG2E
            m = jnp.maximum(jnp.max(jnp.maximum(jnp.maximum(sp, sc), sn), axis=0, keepdims=True), sk)
            ot = None
            for v, sb in zip(vts, (sp, sc, sn)):
                ve = jnp.concatenate([v[hd], ones], axis=0)
                part = _dot(ve, jnp.exp2(sb - m).astype(BF16))
                ot = part if ot is None else ot + part
            den = ot[HEAD_DIM:HEAD_DIM + 1] + jnp.exp2(sk - m)
            ot = ot[0:HEAD_DIM] * (1.0 / den)
            ot = jnp.concatenate([ot[:, hh * blk:(hh + 1) * blk] for hh in range(ATTN_GROUP)], axis=0)
            outs.append(ot.T)
        o_all = jnp.concatenate(outs, axis=1)
        ga = ga_ref[0, pl.ds(row0, blk), :].astype(F32)
        o_ref[0, pl.ds(row0, blk), :] = (o_all * _silu(ga)).astype(BF16)
        return carry

    lax.fori_loop(0, nb, body, 0)


def _attention(sink, qt4, k3, vt4, ga3):
    b, l, _ = k3.shape
    tq = min(ATTN_TILE, l)
    per = tq // ATTN_BLOCK
    nblk = l // ATTN_BLOCK
    cur = lambda bi, n: (bi, n, 0)
    prev, nxt = _halo_maps(per, nblk)
    cur4 = lambda bi, n: (bi, n, 0, 0)
    prev4 = lambda bi, n: prev(bi, n) + (0,)
    nxt4 = lambda bi, n: nxt(bi, n) + (0,)
    return pl.pallas_call(
        functools.partial(_attn_kernel, tq=tq, nblk=nblk),
        grid=(b, l // tq),
        in_specs=[pl.BlockSpec(memory_space=pltpu.SMEM),
                  pl.BlockSpec((1, per, ATTN_Q, ATTN_BLOCK), cur4),
                  pl.BlockSpec((1, tq, ATTN_KV), cur), pl.BlockSpec((1, ATTN_BLOCK, ATTN_KV), prev),
                  pl.BlockSpec((1, ATTN_BLOCK, ATTN_KV), nxt),
                  pl.BlockSpec((1, per, ATTN_KV, ATTN_BLOCK), cur4), pl.BlockSpec((1, 1, ATTN_KV, ATTN_BLOCK), prev4),
                  pl.BlockSpec((1, 1, ATTN_KV, ATTN_BLOCK), nxt4),
                  pl.BlockSpec((1, tq, ATTN_Q), cur)],
        out_specs=pl.BlockSpec((1, tq, ATTN_Q), cur),
        out_shape=jax.ShapeDtypeStruct((b, l, ATTN_Q), BF16),
        scratch_shapes=[pltpu.VMEM((tq + 2 * ATTN_BLOCK, ATTN_KV), BF16),
                        pltpu.VMEM((per + 2, ATTN_KV, ATTN_BLOCK), BF16)],
        compiler_params=_params("parallel", "parallel"),
        name="attention",
    )(sink, qt4, k3, k3, k3, vt4, vt4, vt4, ga3)


def _ssd_decay(dtr, bias, alog, rev):
    ch = SSM_CHUNK
    dt = jax.nn.softplus(dtr + bias)
    a = -jnp.exp(alog) * LOG2E
    v = dt * a
    ri = lax.broadcasted_iota(jnp.int32, (ch, ch), 0)
    ci = lax.broadcasted_iota(jnp.int32, (ch, ch), 1)
    tri = (ci >= ri) if rev else (ci <= ri)
    trib = jnp.where(tri, 1.0, 0.0).astype(BF16)
    v1 = v.astype(BF16)
    r1 = v - v1.astype(F32)
    v2 = r1.astype(BF16)
    v3 = (r1 - v2.astype(F32)).astype(BF16)
    acs = _dot(trib, v1) + _dot(trib, v2) + _dot(trib, v3)
    tot = acs[0:1, :] if rev else acs[ch - 1:ch, :]
    te = jnp.exp2(tot - acs) * dt
    cdec = jnp.exp2(tot)
    acs_t = acs.T
    arow_t = acs_t - jnp.log(dt.T) * LOG2E
    return acs, arow_t, te.T, cdec, tri


def _ssd_local(xbc, decay, rev):
    ch = SSM_CHUNK
    hp = SSM_HEADS // SSM_GROUPS
    gw = hp * SSM_HEAD_DIM
    off = SSM_HEADS if rev else 0
    acs, arow_t, te_t, cdec, tri = decay
    lo = lax.broadcasted_iota(jnp.int32, (ch, LANES), 1) < SSM_HEAD_DIM
    lo1 = lo[0:1, :]

    xs = xbc[:, 0:SSM_INNER]
    bcs = []
    for g in range(SSM_GROUPS):
        bg = xbc[:, SSM_INNER + g * SSM_STATE:SSM_INNER + (g + 1) * SSM_STATE]
        cg = xbc[:, SSM_INNER + (SSM_GROUPS + g) * SSM_STATE:SSM_INNER + (SSM_GROUPS + g + 1) * SSM_STATE]
        bcs.append((cg, _dot_nt(cg, bg), bg.astype(F32).T))
    groups = []
    for g, (cg, cb, bt) in enumerate(bcs):
        eacs_x, cdec_x, yd, sn = [], [], [], []
        for k in range(hp // 2):
            h0 = off + g * hp + 2 * k
            xp = xs[:, g * gw + k * LANES:g * gw + (k + 1) * LANES]
            ms, bs, ea = [], [], []
            for e in range(2):
                h = h0 + e
                acol = jnp.broadcast_to(acs[:, h:h + 1], (ch, ch))
                dec = jnp.exp2(jnp.where(tri, acol - arow_t[h:h + 1, :], -jnp.inf))
                ms.append((cb * dec).astype(BF16))
                bs.append((bt * te_t[h:h + 1, :]).astype(BF16))
                ea.append(jnp.exp2(acol))
            zero = jnp.zeros_like(xp)
            rhs = jnp.concatenate([jnp.where(lo, xp, zero), jnp.where(lo, zero, xp)], axis=0)
            yd.append(_dot(jnp.concatenate(ms, axis=1), rhs))
            sn.append(_dot(jnp.concatenate(bs, axis=1), rhs))
            eacs_x.append(jnp.where(lo, ea[0], ea[1]))
            cdec_x.append(jnp.where(lo1, cdec[:, h0:h0 + 1], cdec[:, h0 + 1:h0 + 2]))
        groups.append((cg, jnp.concatenate(yd, axis=1), jnp.concatenate(sn, axis=1),
                       jnp.concatenate(eacs_x, axis=1), jnp.concatenate(cdec_x, axis=1)))
    return xs, groups


def _ssd_kernel(*refs, rev, nsub):
    if rev:
        (xbc_ref, dt_ref, bias_ref, alog_ref, yf_ref, z_ref, dskip_ref, gn_ref, out_ref, st_ref) = refs
    else:
        (xbc_ref, dt_ref, bias_ref, alog_ref, out_ref, st_ref) = refs
    ch = SSM_CHUNK

    @pl.when(pl.program_id(1) == 0)
    def _():
        st_ref[...] = jnp.zeros_like(st_ref)

    order = list(range(nsub - 1, -1, -1) if rev else range(nsub))
    rows = [slice(sc * ch, (sc + 1) * ch) for sc in range(nsub)]
    decay = {sc: _ssd_decay(dt_ref[0, rows[sc], :], bias_ref[...], alog_ref[...], rev) for sc in order}
    for sc in order:
        xs, groups = _ssd_local(xbc_ref[0, rows[sc], :], decay[sc], rev)
        ys = []
        for g, (cg, yd, sn, eacs_x, cdec_x) in enumerate(groups):
            st_prev = st_ref[g]
            ys.append(yd + _dot(cg, st_prev.astype(BF16)) * eacs_x)
            st_ref[g] = st_prev * cdec_x + sn
        y = jnp.concatenate(ys, axis=1)
        if rev:
            y = y + yf_ref[0, rows[sc], :] + dskip_ref[...] * xs.astype(F32)
            y = y * _silu(z_ref[0, rows[sc], :].astype(F32))
            out_ref[0, rows[sc], :] = _rms(y, gn_ref[...]).astype(BF16)
        else:
            out_ref[0, rows[sc], :] = y


def _ssd(xbcc3, dt3, bias, alog, rev, extra=()):
    b, l, _ = xbcc3.shape
    nsub = min(SSD_STEP, l // SSM_CHUNK)
    nc = l // (nsub * SSM_CHUNK)
    idx = (lambda bi, c: (bi, nc - 1 - c, 0)) if rev else (lambda bi, c: (bi, c, 0))
    blk = lambda w: pl.BlockSpec((1, nsub * SSM_CHUNK, w), idx)
    in_specs = [blk(SSM_XBC), blk(LANES), _const_spec(bias.shape), _const_spec(alog.shape)]
    if rev:
        yf, z3, dskip, gn = extra
        in_specs += [blk(SSM_INNER), blk(SSM_INNER), _const_spec(dskip.shape), _const_spec(gn.shape)]
    return pl.pallas_call(
        functools.partial(_ssd_kernel, rev=rev, nsub=nsub),
        grid=(b, nc),
        in_specs=in_specs,
        out_specs=blk(SSM_INNER),
        out_shape=jax.ShapeDtypeStruct((b, l, SSM_INNER), BF16 if rev else F32),
        scratch_shapes=[pltpu.VMEM((SSM_GROUPS, SSM_STATE, SSM_INNER // SSM_GROUPS), F32)],
        compiler_params=_params("parallel", "arbitrary"),
        name="ssd_bwd" if rev else "ssd_fwd",
    )(xbcc3, dt3, bias, alog, *extra)


def _tail(y, x, p, gpost, wg, wp, gple):
    x1 = x + _rms(y, gpost)
    gate = _sigmoid(_dot(x1.astype(BF16), wg))
    pp = _dot(p.astype(BF16), wp)
    return x1 + _rms(gate * pp, gple)


def _even_out_kernel(oa_ref, ob_ref, x_ref, p_ref, woa_ref, wob_ref, gpost_ref, wg_ref, wp_ref, gple_ref,
                     out_ref):
    y = _dot(oa_ref[...], woa_ref[...]) + _dot(ob_ref[...], wob_ref[...])
    out_ref[...] = _tail(y, x_ref[...], p_ref[...], gpost_ref[...], wg_ref[...], wp_ref[...], gple_ref[...])


def _even_out(oa, ob, x2, p3, layer, woa, wob, gpost, wg, wp, gple, seq_len):
    t = x2.shape[0]
    tm = min(WIDE_TILE, seq_len)
    row = lambda n: (n, 0)
    consts = (woa, wob, gpost, wg, wp, gple)
    return pl.pallas_call(
        _even_out_kernel,
        grid=(t // tm,),
        in_specs=[pl.BlockSpec((tm, ATTN_Q), row), pl.BlockSpec((tm, SSM_INNER), row),
                  pl.BlockSpec((tm, D_MODEL), row), pl.BlockSpec((None, tm, PLE_DIM), lambda n: (layer, n, 0))]
                 + [_const_spec(c.shape) for c in consts],
        out_specs=pl.BlockSpec((tm, D_MODEL), row),
        out_shape=jax.ShapeDtypeStruct((t, D_MODEL), F32),
        compiler_params=_params("parallel"),
        name="even_out",
    )(oa, ob, x2, p3, *consts)


def _odd_in_kernel(x_ref, g_ref, wa_ref, wb_ref, wg_ref, h_ref, sg_ref):
    u = _rms(x_ref[...], g_ref[...]).astype(BF16)
    a = _dot(u, wa_ref[...])
    b = _dot(u, wb_ref[...])
    h_ref[...] = (a * _sigmoid(b)).astype(BF16)
    sg_ref[...] = _silu(_dot(u, wg_ref[...])).astype(BF16)


def _odd_in(x2, g, wa, wb, wg, seq_len):
    t = x2.shape[0]
    tm = min(WIDE_TILE, seq_len)
    row = lambda n: (n, 0)
    return pl.pallas_call(
        _odd_in_kernel,
        grid=(t // tm,),
        in_specs=[pl.BlockSpec((tm, D_MODEL), row), _const_spec(g.shape), _const_spec(wa.shape),
                  _const_spec(wb.shape), _const_spec(wg.shape)],
        out_specs=[pl.BlockSpec((tm, CONV_INNER), row), pl.BlockSpec((tm, CONV_INNER), row)],
        out_shape=[jax.ShapeDtypeStruct((t, CONV_INNER), BF16), jax.ShapeDtypeStruct((t, CONV_INNER), BF16)],
        compiler_params=_params("parallel"),
        name="odd_in",
    )(x2, g, wa, wb, wg)


def _odd_out_kernel(hc_ref, hp_ref, hn_ref, sg_ref, x_ref, p_ref, cw_ref, cb_ref, lng_ref, lnb_ref, wo_ref,
                    gpost_ref, wg_ref, wp_ref, gple_ref, out_ref, pad_ref, sh_ref, wb_ref, cv_ref, *, tc, ntile):
    n = pl.program_id(1)
    pad_ref[0:HALO] = jnp.where(n > 0, hp_ref[0].astype(F32), 0.0)
    pad_ref[HALO:HALO + tc] = hc_ref[0].astype(F32)
    pad_ref[HALO + tc:] = jnp.where(n < ntile - 1, hn_ref[0].astype(F32), 0.0)
    half = CONV_WIDTH // 2
    rows = CONV_ROWS
    span = sh_ref.shape[1]
    for c in range(SUBLANES):
        sh_ref[c] = pad_ref[pl.ds(c, span), :]
    for k in range(CONV_WIDTH):
        wb_ref[k] = jnp.broadcast_to(cw_ref[k:k + 1, :], (SUBLANES, CONV_INNER))

    def body(i, carry):
        r0 = pl.multiple_of(i * rows, rows)
        for l0 in range(0, CONV_INNER, CONV_LANES):
            ls = slice(l0, l0 + CONV_LANES)
            accs = [jnp.zeros((SUBLANES, CONV_LANES), F32) + cb_ref[:, ls] for _ in range(rows // SUBLANES)]
            for k in range(CONV_WIDTH):
                o = HALO - half + k
                w = wb_ref[k, :, ls]
                for r in range(rows // SUBLANES):
                    a0 = r0 + o - o % SUBLANES + r * SUBLANES
                    accs[r] = accs[r] + sh_ref[o % SUBLANES, pl.ds(a0, SUBLANES), ls] * w
            for r in range(rows // SUBLANES):
                cv_ref[pl.ds(r0 + r * SUBLANES, SUBLANES), ls] = accs[r]
        return carry

    lax.fori_loop(0, tc // rows, body, 0)
    cv = cv_ref[...]
    mu = jnp.mean(cv, axis=-1, keepdims=True)
    cen = cv - mu
    var = jnp.mean(cen * cen, axis=-1, keepdims=True)
    hn = _silu(cen * lax.rsqrt(var + NORM_EPS) * lng_ref[...] + lnb_ref[...])
    hm = (hn * sg_ref[0].astype(F32)).astype(BF16)
    y = _dot(hm, wo_ref[...])
    out_ref[0] = _tail(y, x_ref[0], p_ref[0], gpost_ref[...], wg_ref[...], wp_ref[...], gple_ref[...])


def _odd_out(h3, sg3, x3, p4, layer, cw, cb, lng, lnb, wo, gpost, wg, wp, gple):
    b, l, _ = h3.shape
    tc = min(CONV_TILE, l)
    ntile = l // tc
    cur = lambda bi, n: (bi, n, 0)
    prev, nxt = _halo_maps(tc // HALO, l // HALO)
    consts = (cw, cb, lng, lnb, wo, gpost, wg, wp, gple)
    return pl.pallas_call(
        functools.partial(_odd_out_kernel, tc=tc, ntile=ntile),
        grid=(b, ntile),
        in_specs=[pl.BlockSpec((1, tc, CONV_INNER), cur), pl.BlockSpec((1, HALO, CONV_INNER), prev),
                  pl.BlockSpec((1, HALO, CONV_INNER), nxt), pl.BlockSpec((1, tc, CONV_INNER), cur),
                  pl.BlockSpec((1, tc, D_MODEL), cur),
                  pl.BlockSpec((None, 1, tc, PLE_DIM), lambda bi, n: (layer, bi, n, 0))]
                 + [_const_spec(c.shape) for c in consts],
        out_specs=pl.BlockSpec((1, tc, D_MODEL), cur),
        out_shape=jax.ShapeDtypeStruct((b, l, D_MODEL), F32),
        scratch_shapes=[pltpu.VMEM((tc + 2 * HALO, CONV_INNER), F32),
                        pltpu.VMEM((SUBLANES, tc + 2 * HALO - SUBLANES, CONV_INNER), F32),
                        pltpu.VMEM((CONV_WIDTH, SUBLANES, CONV_INNER), F32),
                        pltpu.VMEM((tc, CONV_INNER), F32)],
        compiler_params=_params("parallel", "parallel"),
        name="odd_out",
    )(h3, h3, h3, sg3, x3, p4, *consts)


def _rope_tables(length):
    inv = 1.0 / (jnp.float32(ROPE_THETA) ** (jnp.arange(0, ROT_DIM, 2, dtype=F32) / ROT_DIM))
    ang = jnp.arange(length, dtype=F32)[:, None] * inv[None, :]
    cos, sin = jnp.cos(ang), jnp.sin(ang)
    m = jnp.arange(LANES) % HEAD_DIM
    idx = m % ROT_HALF
    cos_l = jnp.where(m < ROT_DIM, cos[:, idx], 1.0)
    sin_l = sin[:, idx]
    sa = jnp.where((m >= ROT_HALF) & (m < ROT_DIM), sin_l, 0.0)
    sb = jnp.where(m < ROT_HALF, -sin_l, 0.0)
    return cos_l, sa, sb, cos.T, sin.T


def _row(v):
    return v.reshape(1, -1).astype(F32)


def _pad_rows(w, rows):
    return jnp.concatenate([w, jnp.zeros((rows - w.shape[0],) + w.shape[1:], w.dtype)], axis=0)


def _pad_lanes(v, lanes=LANES):
    v = v.reshape(1, -1).astype(F32)
    return jnp.concatenate([v, jnp.zeros((1, lanes - v.shape[1]), F32)], axis=1)


def _even_weights(j, ev_w_in, ev_w_out, attn_sink, ssm_conv_w, ssm_conv_b, ssm_dt_bias, ssm_a_log, ssm_d,
                  ssm_norm):
    w = ev_w_in[j].astype(BF16)
    c0 = 0
    cols = []
    for width in (ATTN_Q, ATTN_KV, ATTN_KV, ATTN_Q, SSM_XBC, SSM_INNER, 2 * SSM_HEADS):
        cols.append(w[:, c0:c0 + width])
        c0 += width
    wq, wk, wv, wga, wxbc, wz, wdt = cols
    wdt = jnp.concatenate([wdt, jnp.zeros((D_MODEL, LANES - 2 * SSM_HEADS), BF16)], axis=1)
    wo = ev_w_out[j].astype(BF16)
    return dict(
        wqt=wq.T, wk=wk, wvt=wv.T, wga=wga, wxbc=wxbc, wz=wz, wdt=wdt, woa=wo[:ATTN_Q], wob=wo[ATTN_Q:],
        sink=attn_sink[j].astype(F32),
        conv_w=_pad_rows(ssm_conv_w[j].astype(F32), SUBLANES), conv_b=_row(ssm_conv_b[j]),
        dt_bias=_pad_lanes(ssm_dt_bias[j]), a_log=_pad_lanes(ssm_a_log[j]),
        dskip=_row(jnp.repeat(ssm_d[j].astype(F32), SSM_HEAD_DIM)), ssm_norm=_row(ssm_norm[j]))


def _odd_weights(j, od_w_in, od_conv_w, od_conv_b, od_ln_g, od_ln_b, od_w_out):
    w = od_w_in[j].astype(BF16)
    return dict(
        wa=w[:, :CONV_INNER], wb=w[:, CONV_INNER:2 * CONV_INNER], wg=w[:, 2 * CONV_INNER:],
        conv_w=_pad_rows(od_conv_w[j].astype(F32), 4 * SUBLANES), conv_b=_row(od_conv_b[j]),
        ln_g=_row(od_ln_g[j]), ln_b=_row(od_ln_b[j]), wo=od_w_out[j].astype(BF16))


def _trunk(x, p, layers, rope):
    b, l, _ = x.shape
    t = b * l
    flat = lambda a: a.reshape(t, a.shape[-1])
    seq = lambda a: a.reshape(b, l, a.shape[-1])
    p3 = p.reshape(p.shape[0], t, PLE_DIM)
    for i, lw in enumerate(layers):
        common = (lw["gpost"], lw["ple_wg"], lw["ple_wp"], lw["gple"])
        if i % 2 == 0:
            qt, k, vt, ga, xbcc, z, dt = _even_in(x, lw["gpre"], lw, rope)
            o_attn = _attention(lw["sink"], qt, k, vt, ga)
            y_f = _ssd(xbcc, dt, lw["dt_bias"], lw["a_log"], rev=False)
            o_ssm = _ssd(xbcc, dt, lw["dt_bias"], lw["a_log"], rev=True,
                         extra=(y_f, z, lw["dskip"], lw["ssm_norm"]))
            x = seq(_even_out(flat(o_attn), flat(o_ssm), flat(x), p3, i, lw["woa"], lw["wob"], *common, l))
        else:
            h, sg = _odd_in(flat(x), lw["gpre"], lw["wa"], lw["wb"], lw["wg"], l)
            x = _odd_out(seq(h), seq(sg), x, p, i, lw["conv_w"], lw["conv_b"], lw["ln_g"], lw["ln_b"], lw["wo"],
                         *common)
    return x


def kernel(x_prompt, x_sample, p_prompt, p_sample, norm_pre, norm_post, ple_w_gate, ple_w_proj, ple_norm, ev_w_in, ev_w_out, attn_sink, ssm_conv_w, ssm_conv_b, ssm_dt_bias, ssm_a_log, ssm_d, ssm_norm, od_w_in, od_conv_w, od_conv_b, od_ln_g, od_ln_b, od_w_out):
    layers = []
    for i in range(DEPTH):
        j = i // 2
        if i % 2 == 0:
            lw = _even_weights(j, ev_w_in, ev_w_out, attn_sink, ssm_conv_w, ssm_conv_b, ssm_dt_bias, ssm_a_log,
                               ssm_d, ssm_norm)
        else:
            lw = _odd_weights(j, od_w_in, od_conv_w, od_conv_b, od_ln_g, od_ln_b, od_w_out)
        lw.update(gpre=_row(norm_pre[i]), gpost=_row(norm_post[i]), ple_wg=ple_w_gate[i].astype(BF16),
                  ple_wp=ple_w_proj[i].astype(BF16), gple=_row(ple_norm[i]))
        layers.append(lw)
    y_prompt = _trunk(x_prompt, p_prompt, layers, _rope_tables(x_prompt.shape[1]))
    y_sample = _trunk(x_sample, p_sample, layers, _rope_tables(x_sample.shape[1]))
    return (y_prompt, y_sample)
```

```python
import functools

import jax
import jax.numpy as jnp
from jax import lax
from jax.experimental import pallas as pl
from jax.experimental.pallas import tpu as pltpu

F32 = jnp.float32
BF16 = jnp.bfloat16

D_MODEL = 1024
DEPTH = 4
PLE_DIM = 256
NORM_EPS = 1e-6
ATTN_HEADS = 16
ATTN_KV_HEADS = 4
HEAD_DIM = 64
ATTN_GROUP = ATTN_HEADS // ATTN_KV_HEADS
ATTN_BLOCK = 128
ROPE_THETA = 500000.0
ROT_DIM = HEAD_DIM // 4
ROT_HALF = ROT_DIM // 2
ATTN_Q = ATTN_HEADS * HEAD_DIM
ATTN_KV = ATTN_KV_HEADS * HEAD_DIM
SSM_INNER = D_MODEL
SSM_HEAD_DIM = 64
SSM_HEADS = SSM_INNER // SSM_HEAD_DIM
SSM_GROUPS = 2
SSM_STATE = 128
SSM_CONV = 5
SSM_CHUNK = 128
SSM_BC = 2 * SSM_GROUPS * SSM_STATE
SSM_XBC = SSM_INNER + SSM_BC
CONV_INNER = D_MODEL
CONV_WIDTH = 31

LANES = 128
SUBLANES = 8
HALO = 16
VMEM_LIMIT = 56 * 1024 * 1024

TOKEN_TILE = 512
WIDE_TILE = 1024
ATTN_TILE = 512
CONV_TILE = 512
SSD_STEP = 8
XBC_SLABS = 3
CONV_ROWS = 32
CONV_LANES = 512

NT_DIMS = (((1,), (1,)), ((), ()))
LOG2E = 1.4426950408889634


def _params(*sem):
    return pltpu.CompilerParams(dimension_semantics=sem, vmem_limit_bytes=VMEM_LIMIT)


def _const_spec(shape):
    nd = len(shape)
    return pl.BlockSpec(shape, lambda *_: (0,) * nd, pipeline_mode=pl.Buffered(1))


def _rms(x, g):
    ms = jnp.mean(x * x, axis=-1, keepdims=True)
    return x * lax.rsqrt(ms + NORM_EPS) * g


def _sigmoid(x):
    return jax.nn.sigmoid(x)


def _silu(x):
    return x * _sigmoid(x)


def _dot(a, b):
    return jnp.dot(a, b, preferred_element_type=F32)


def _dot_nt(a, b):
    return lax.dot_general(a, b, NT_DIMS, preferred_element_type=F32)


def _halo_maps(per, nh):
    prev = lambda bi, n: (bi, jnp.maximum(n * per - 1, 0), 0)
    nxt = lambda bi, n: (bi, jnp.minimum((n + 1) * per, nh - 1), 0)
    return prev, nxt


def _even_in_kernel(xc_ref, xp_ref, xn_ref, g_ref, wqt_ref, wk_ref, wvt_ref, wga_ref, wxbc_ref, wz_ref, wdt_ref,
                    cos_ref, sa_ref, sb_ref, cost_ref, sint_ref, cw_ref, cb_ref,
                    qt_ref, k_ref, vt_ref, ga_ref, xbc_ref, z_ref, dt_ref, *pad_refs, tm, ntile):
    n = pl.program_id(1)
    g = g_ref[...]
    u = _rms(xc_ref[0], g).astype(BF16)

    xh = jnp.concatenate([xp_ref[0], xn_ref[0]], axis=0)
    uh = _rms(xh, g).astype(BF16)
    sw = SSM_XBC // len(pad_refs)
    for c, pad_ref in enumerate(pad_refs):
        w = wxbc_ref[:, c * sw:(c + 1) * sw]
        hal = _dot(uh, w)
        pad_ref[0:SUBLANES] = jnp.where(n > 0, hal[0:SUBLANES], 0.0)
        pad_ref[SUBLANES:SUBLANES + tm] = _dot(u, w)
        pad_ref[SUBLANES + tm:] = jnp.where(n < ntile - 1, hal[SUBLANES:], 0.0)

    qt = _dot_nt(wqt_ref[...], u)
    ct = cost_ref[...]
    st = sint_ref[...]
    parts = []
    for h in range(ATTN_HEADS):
        b0 = h * HEAD_DIM
        r1 = qt[b0:b0 + ROT_HALF]
        r2 = qt[b0 + ROT_HALF:b0 + ROT_DIM]
        parts += [r1 * ct - r2 * st, r2 * ct + r1 * st, qt[b0 + ROT_DIM:b0 + HEAD_DIM]]
    scale = HEAD_DIM ** -0.5 * LOG2E
    qt = (jnp.concatenate(parts, axis=0) * scale).astype(BF16)
    vt = _dot_nt(wvt_ref[...], u).astype(BF16)
    for j in range(tm // ATTN_BLOCK):
        sl = slice(j * ATTN_BLOCK, (j + 1) * ATTN_BLOCK)
        qt_ref[0, j] = qt[:, sl]
        vt_ref[0, j] = vt[:, sl]

    cos = cos_ref[...]
    sa = sa_ref[...]
    sb = sb_ref[...]
    k = _dot(u, wk_ref[...])
    for j in range(ATTN_KV // LANES):
        sl = slice(j * LANES, (j + 1) * LANES)
        t = k[:, sl]
        k_ref[0, :, sl] = (t * cos + pltpu.roll(t, ROT_HALF, 1) * sa
                           + pltpu.roll(t, LANES - ROT_HALF, 1) * sb).astype(BF16)

    ga_ref[0] = _dot(u, wga_ref[...]).astype(BF16)
    z_ref[0] = _dot(u, wz_ref[...]).astype(BF16)
    dt_ref[0] = _dot(u, wdt_ref[...])

    half = SSM_CONV // 2
    for c, pad_ref in enumerate(pad_refs):
        cs = slice(c * sw, (c + 1) * sw)
        acc = jnp.zeros((tm, sw), F32) + cb_ref[:, cs]
        for kk in range(SSM_CONV):
            acc = acc + pad_ref[pl.ds(SUBLANES - half + kk, tm), :] * cw_ref[kk:kk + 1, cs]
        xbc_ref[0, :, cs] = _silu(acc).astype(BF16)


def _even_in(x3, g, lw, rope):
    b, l, _ = x3.shape
    tm = min(TOKEN_TILE, l)
    ntile = l // tm
    nb = tm // ATTN_BLOCK
    cos, sa, sb, cos_t, sin_t = rope
    cur = lambda bi, n: (bi, n, 0)
    prev, nxt = _halo_maps(tm // SUBLANES, l // SUBLANES)
    cur4 = lambda bi, n: (bi, n, 0, 0)
    pos = lambda bi, n: (n, 0)
    pos_t = lambda bi, n: (0, n)
    consts = (g, lw["wqt"], lw["wk"], lw["wvt"], lw["wga"], lw["wxbc"], lw["wz"], lw["wdt"])
    tail = (lw["conv_w"], lw["conv_b"])
    out_shape = [jax.ShapeDtypeStruct((b, l // ATTN_BLOCK, ATTN_Q, ATTN_BLOCK), BF16),
                 jax.ShapeDtypeStruct((b, l, ATTN_KV), BF16),
                 jax.ShapeDtypeStruct((b, l // ATTN_BLOCK, ATTN_KV, ATTN_BLOCK), BF16),
                 jax.ShapeDtypeStruct((b, l, ATTN_Q), BF16),
                 jax.ShapeDtypeStruct((b, l, SSM_XBC), BF16),
                 jax.ShapeDtypeStruct((b, l, SSM_INNER), BF16),
                 jax.ShapeDtypeStruct((b, l, LANES), F32)]
    out_specs = [pl.BlockSpec((1, nb, ATTN_Q, ATTN_BLOCK), cur4), pl.BlockSpec((1, tm, ATTN_KV), cur),
                 pl.BlockSpec((1, nb, ATTN_KV, ATTN_BLOCK), cur4), pl.BlockSpec((1, tm, ATTN_Q), cur),
                 pl.BlockSpec((1, tm, SSM_XBC), cur), pl.BlockSpec((1, tm, SSM_INNER), cur),
                 pl.BlockSpec((1, tm, LANES), cur)]
    return pl.pallas_call(
        functools.partial(_even_in_kernel, tm=tm, ntile=ntile),
        grid=(b, ntile),
        in_specs=[pl.BlockSpec((1, tm, D_MODEL), cur), pl.BlockSpec((1, SUBLANES, D_MODEL), prev),
                  pl.BlockSpec((1, SUBLANES, D_MODEL), nxt)]
                 + [_const_spec(c.shape) for c in consts]
                 + [pl.BlockSpec((tm, LANES), pos)] * 3 + [pl.BlockSpec((ROT_HALF, tm), pos_t)] * 2
                 + [_const_spec(c.shape) for c in tail],
        out_specs=out_specs,
        out_shape=out_shape,
        scratch_shapes=[pltpu.VMEM((tm + 2 * SUBLANES, SSM_XBC // XBC_SLABS), F32)] * XBC_SLABS,
        compiler_params=_params("parallel", "parallel"),
        name="even_in",
    )(x3, x3, x3, *consts, cos, sa, sb, cos_t, sin_t, *tail)


def _attn_kernel(sink_ref, qt_ref, kc_ref, kp_ref, kn_ref, vc_ref, vp_ref, vn_ref, ga_ref, o_ref,
                 kw_ref, vw_ref, *, tq, nblk):
    n = pl.program_id(1)
    blk = ATTN_BLOCK
    nb = tq // blk
    kw_ref[0:blk] = kp_ref[0]
    kw_ref[blk:blk + tq] = kc_ref[0]
    kw_ref[blk + tq:] = kn_ref[0]
    vw_ref[0] = vp_ref[0, 0]
    vw_ref[1:nb + 1] = vc_ref[0]
    vw_ref[nb + 1] = vn_ref[0, 0]
    kpos = lax.broadcasted_iota(jnp.int32, (blk, blk), 0)
    qpos = lax.broadcasted_iota(jnp.int32, (blk, blk), 1)
    lane = lax.broadcasted_iota(jnp.int32, (1, ATTN_GROUP * blk), 1)
    ones = jnp.ones((HALO, blk), BF16)
    neg = F32(-1e30)

    def body(j, carry):
        qb = n * nb + j
        vp = jnp.concatenate([(kpos >= qpos) & (qb > 0)] * ATTN_GROUP, axis=1)
        vn = jnp.concatenate([(kpos <= qpos) & (qb < nblk - 1)] * ATTN_GROUP, axis=1)
        row0 = pl.multiple_of(j * blk, blk)
        qt = qt_ref[0, j]
        kwin = kw_ref[pl.ds(row0, 3 * blk), :]
        vts = (vw_ref[j], vw_ref[j + 1], vw_ref[j + 2])

        def scores(g):
            q4 = jnp.concatenate(
                [qt[(g * ATTN_GROUP + hh) * HEAD_DIM:(g * ATTN_GROUP + hh + 1) * HEAD_DIM] for hh in range(ATTN_GROUP)],
                axis=1)
            kg = kwin[:, g * HEAD_DIM:(g + 1) * HEAD_DIM]
            return (jnp.where(vp, _dot(kg[0:blk], q4), neg),
                    _dot(kg[blk:2 * blk], q4),
                    jnp.where(vn, _dot(kg[2 * blk:], q4), neg))

        pend = [scores(g) for g in range(ATTN_KV_HEADS)]
        outs = []
        for g in range(ATTN_KV_HEADS):
            hd = slice(g * HEAD_DIM, (g + 1) * HEAD_DIM)
            sp, sc, sn = pend[g]
            sk = jnp.full((1, ATTN_GROUP * blk), sink_ref[g * ATTN_GROUP], F32)
            for hh in range(1, ATTN_GROUP):
                sk = jnp.where(lane >= hh * blk, sink_ref[g * ATTN_GROUP + hh], sk)
            sk = sk * LOG2E
            m = jnp.maximum(jnp.max(jnp.maximum(jnp.maximum(sp, sc), sn), axis=0, keepdims=True), sk)
            ot = None
            for v, sb in zip(vts, (sp, sc, sn)):
                ve = jnp.concatenate([v[hd], ones], axis=0)
                part = _dot(ve, jnp.exp2(sb - m).astype(BF16))
                ot = part if ot is None else ot + part
            den = ot[HEAD_DIM:HEAD_DIM + 1] + jnp.exp2(sk - m)
            ot = ot[0:HEAD_DIM] * (1.0 / den)
            ot = jnp.concatenate([ot[:, hh * blk:(hh + 1) * blk] for hh in range(ATTN_GROUP)], axis=0)
            outs.append(ot.T)
        o_all = jnp.concatenate(outs, axis=1)
        ga = ga_ref[0, pl.ds(row0, blk), :].astype(F32)
        o_ref[0, pl.ds(row0, blk), :] = (o_all * _silu(ga)).astype(BF16)
        return carry

    lax.fori_loop(0, nb, body, 0)


def _attention(sink, qt4, k3, vt4, ga3):
    b, l, _ = k3.shape
    tq = min(ATTN_TILE, l)
    per = tq // ATTN_BLOCK
    nblk = l // ATTN_BLOCK
    cur = lambda bi, n: (bi, n, 0)
    prev, nxt = _halo_maps(per, nblk)
    cur4 = lambda bi, n: (bi, n, 0, 0)
    prev4 = lambda bi, n: prev(bi, n) + (0,)
    nxt4 = lambda bi, n: nxt(bi, n) + (0,)
    return pl.pallas_call(
        functools.partial(_attn_kernel, tq=tq, nblk=nblk),
        grid=(b, l // tq),
        in_specs=[pl.BlockSpec(memory_space=pltpu.SMEM),
                  pl.BlockSpec((1, per, ATTN_Q, ATTN_BLOCK), cur4),
                  pl.BlockSpec((1, tq, ATTN_KV), cur), pl.BlockSpec((1, ATTN_BLOCK, ATTN_KV), prev),
                  pl.BlockSpec((1, ATTN_BLOCK, ATTN_KV), nxt),
                  pl.BlockSpec((1, per, ATTN_KV, ATTN_BLOCK), cur4), pl.BlockSpec((1, 1, ATTN_KV, ATTN_BLOCK), prev4),
                  pl.BlockSpec((1, 1, ATTN_KV, ATTN_BLOCK), nxt4),
                  pl.BlockSpec((1, tq, ATTN_Q), cur)],
        out_specs=pl.BlockSpec((1, tq, ATTN_Q), cur),
        out_shape=jax.ShapeDtypeStruct((b, l, ATTN_Q), BF16),
        scratch_shapes=[pltpu.VMEM((tq + 2 * ATTN_BLOCK, ATTN_KV), BF16),
                        pltpu.VMEM((per + 2, ATTN_KV, ATTN_BLOCK), BF16)],
        compiler_params=_params("parallel", "parallel"),
        name="attention",
    )(sink, qt4, k3, k3, k3, vt4, vt4, vt4, ga3)


def _ssd_decay(dtr, bias, alog, rev):
    ch = SSM_CHUNK
    dt = jax.nn.softplus(dtr + bias)
    a = -jnp.exp(alog) * LOG2E
    v = dt * a
    ri = lax.broadcasted_iota(jnp.int32, (ch, ch), 0)
    ci = lax.broadcasted_iota(jnp.int32, (ch, ch), 1)
    tri = (ci >= ri) if rev else (ci <= ri)
    trib = jnp.where(tri, 1.0, 0.0).astype(BF16)
    v1 = v.astype(BF16)
    r1 = v - v1.astype(F32)
    v2 = r1.astype(BF16)
    v3 = (r1 - v2.astype(F32)).astype(BF16)
    acs = _dot(trib, v1) + _dot(trib, v2) + _dot(trib, v3)
    tot = acs[0:1, :] if rev else acs[ch - 1:ch, :]
    te = jnp.exp2(tot - acs) * dt
    cdec = jnp.exp2(tot)
    acs_t = acs.T
    arow_t = acs_t - jnp.log(dt.T) * LOG2E
    return acs, arow_t, te.T, cdec, tri


def _ssd_local(xbc, decay, rev):
    ch = SSM_CHUNK
    hp = SSM_HEADS // SSM_GROUPS
    gw = hp * SSM_HEAD_DIM
    off = SSM_HEADS if rev else 0
    acs, arow_t, te_t, cdec, tri = decay
    lo = lax.broadcasted_iota(jnp.int32, (ch, LANES), 1) < SSM_HEAD_DIM
    lo1 = lo[0:1, :]

    xs = xbc[:, 0:SSM_INNER]
    bcs = []
    for g in range(SSM_GROUPS):
        bg = xbc[:, SSM_INNER + g * SSM_STATE:SSM_INNER + (g + 1) * SSM_STATE]
        cg = xbc[:, SSM_INNER + (SSM_GROUPS + g) * SSM_STATE:SSM_INNER + (SSM_GROUPS + g + 1) * SSM_STATE]
        bcs.append((cg, _dot_nt(cg, bg), bg.astype(F32).T))
    groups = []
    for g, (cg, cb, bt) in enumerate(bcs):
        eacs_x, cdec_x, yd, sn = [], [], [], []
        for k in range(hp // 2):
            h0 = off + g * hp + 2 * k
            xp = xs[:, g * gw + k * LANES:g * gw + (k + 1) * LANES]
            ms, bs, ea = [], [], []
            for e in range(2):
                h = h0 + e
                acol = jnp.broadcast_to(acs[:, h:h + 1], (ch, ch))
                dec = jnp.exp2(jnp.where(tri, acol - arow_t[h:h + 1, :], -jnp.inf))
                ms.append((cb * dec).astype(BF16))
                bs.append((bt * te_t[h:h + 1, :]).astype(BF16))
                ea.append(jnp.exp2(acol))
            zero = jnp.zeros_like(xp)
            rhs = jnp.concatenate([jnp.where(lo, xp, zero), jnp.where(lo, zero, xp)], axis=0)
            yd.append(_dot(jnp.concatenate(ms, axis=1), rhs))
            sn.append(_dot(jnp.concatenate(bs, axis=1), rhs))
            eacs_x.append(jnp.where(lo, ea[0], ea[1]))
            cdec_x.append(jnp.where(lo1, cdec[:, h0:h0 + 1], cdec[:, h0 + 1:h0 + 2]))
        groups.append((cg, jnp.concatenate(yd, axis=1), jnp.concatenate(sn, axis=1),
                       jnp.concatenate(eacs_x, axis=1), jnp.concatenate(cdec_x, axis=1)))
    return xs, groups


def _ssd_kernel(*refs, rev, nsub):
    if rev:
        (xbc_ref, dt_ref, bias_ref, alog_ref, yf_ref, z_ref, dskip_ref, gn_ref, out_ref, st_ref) = refs
    else:
        (xbc_ref, dt_ref, bias_ref, alog_ref, out_ref, st_ref) = refs
    ch = SSM_CHUNK

    @pl.when(pl.program_id(1) == 0)
    def _():
        st_ref[...] = jnp.zeros_like(st_ref)

    order = list(range(nsub - 1, -1, -1) if rev else range(nsub))
    rows = [slice(sc * ch, (sc + 1) * ch) for sc in range(nsub)]
    decay = {sc: _ssd_decay(dt_ref[0, rows[sc], :], bias_ref[...], alog_ref[...], rev) for sc in order}
    for sc in order:
        xs, groups = _ssd_local(xbc_ref[0, rows[sc], :], decay[sc], rev)
        ys = []
        for g, (cg, yd, sn, eacs_x, cdec_x) in enumerate(groups):
            st_prev = st_ref[g]
            ys.append(yd + _dot(cg, st_prev.astype(BF16)) * eacs_x)
            st_ref[g] = st_prev * cdec_x + sn
        y = jnp.concatenate(ys, axis=1)
        if rev:
            y = y + yf_ref[0, rows[sc], :] + dskip_ref[...] * xs.astype(F32)
            y = y * _silu(z_ref[0, rows[sc], :].astype(F32))
            out_ref[0, rows[sc], :] = _rms(y, gn_ref[...]).astype(BF16)
        else:
            out_ref[0, rows[sc], :] = y


def _ssd(xbcc3, dt3, bias, alog, rev, extra=()):
    b, l, _ = xbcc3.shape
    nsub = min(SSD_STEP, l // SSM_CHUNK)
    nc = l // (nsub * SSM_CHUNK)
    idx = (lambda bi, c: (bi, nc - 1 - c, 0)) if rev else (lambda bi, c: (bi, c, 0))
    blk = lambda w: pl.BlockSpec((1, nsub * SSM_CHUNK, w), idx)
    in_specs = [blk(SSM_XBC), blk(LANES), _const_spec(bias.shape), _const_spec(alog.shape)]
    if rev:
        yf, z3, dskip, gn = extra
        in_specs += [blk(SSM_INNER), blk(SSM_INNER), _const_spec(dskip.shape), _const_spec(gn.shape)]
    return pl.pallas_call(
        functools.partial(_ssd_kernel, rev=rev, nsub=nsub),
        grid=(b, nc),
        in_specs=in_specs,
        out_specs=blk(SSM_INNER),
        out_shape=jax.ShapeDtypeStruct((b, l, SSM_INNER), BF16 if rev else F32),
        scratch_shapes=[pltpu.VMEM((SSM_GROUPS, SSM_STATE, SSM_INNER // SSM_GROUPS), F32)],
        compiler_params=_params("parallel", "arbitrary"),
        name="ssd_bwd" if rev else "ssd_fwd",
    )(xbcc3, dt3, bias, alog, *extra)


def _tail(y, x, p, gpost, wg, wp, gple):
    x1 = x + _rms(y, gpost)
    gate = _sigmoid(_dot(x1.astype(BF16), wg))
    pp = _dot(p.astype(BF16), wp)
    return x1 + _rms(gate * pp, gple)


def _even_out_kernel(oa_ref, ob_ref, x_ref, p_ref, woa_ref, wob_ref, gpost_ref, wg_ref, wp_ref, gple_ref,
                     out_ref):
    y = _dot(oa_ref[...], woa_ref[...]) + _dot(ob_ref[...], wob_ref[...])
    out_ref[...] = _tail(y, x_ref[...], p_ref[...], gpost_ref[...], wg_ref[...], wp_ref[...], gple_ref[...])


def _even_out(oa, ob, x2, p3, layer, woa, wob, gpost, wg, wp, gple, seq_len):
    t = x2.shape[0]
    tm = min(WIDE_TILE, seq_len)
    row = lambda n: (n, 0)
    consts = (woa, wob, gpost, wg, wp, gple)
    return pl.pallas_call(
        _even_out_kernel,
        grid=(t // tm,),
        in_specs=[pl.BlockSpec((tm, ATTN_Q), row), pl.BlockSpec((tm, SSM_INNER), row),
                  pl.BlockSpec((tm, D_MODEL), row), pl.BlockSpec((None, tm, PLE_DIM), lambda n: (layer, n, 0))]
                 + [_const_spec(c.shape) for c in consts],
        out_specs=pl.BlockSpec((tm, D_MODEL), row),
        out_shape=jax.ShapeDtypeStruct((t, D_MODEL), F32),
        compiler_params=_params("parallel"),
        name="even_out",
    )(oa, ob, x2, p3, *consts)


def _odd_in_kernel(x_ref, g_ref, wa_ref, wb_ref, wg_ref, h_ref, sg_ref):
    u = _rms(x_ref[...], g_ref[...]).astype(BF16)
    a = _dot(u, wa_ref[...])
    b = _dot(u, wb_ref[...])
    h_ref[...] = (a * _sigmoid(b)).astype(BF16)
    sg_ref[...] = _silu(_dot(u, wg_ref[...])).astype(BF16)


def _odd_in(x2, g, wa, wb, wg, seq_len):
    t = x2.shape[0]
    tm = min(WIDE_TILE, seq_len)
    row = lambda n: (n, 0)
    return pl.pallas_call(
        _odd_in_kernel,
        grid=(t // tm,),
        in_specs=[pl.BlockSpec((tm, D_MODEL), row), _const_spec(g.shape), _const_spec(wa.shape),
                  _const_spec(wb.shape), _const_spec(wg.shape)],
        out_specs=[pl.BlockSpec((tm, CONV_INNER), row), pl.BlockSpec((tm, CONV_INNER), row)],
        out_shape=[jax.ShapeDtypeStruct((t, CONV_INNER), BF16), jax.ShapeDtypeStruct((t, CONV_INNER), BF16)],
        compiler_params=_params("parallel"),
        name="odd_in",
    )(x2, g, wa, wb, wg)


def _odd_out_kernel(hc_ref, hp_ref, hn_ref, sg_ref, x_ref, p_ref, cw_ref, cb_ref, lng_ref, lnb_ref, wo_ref,
                    gpost_ref, wg_ref, wp_ref, gple_ref, out_ref, pad_ref, sh_ref, wb_ref, cv_ref, *, tc, ntile):
    n = pl.program_id(1)
    pad_ref[0:HALO] = jnp.where(n > 0, hp_ref[0].astype(F32), 0.0)
    pad_ref[HALO:HALO + tc] = hc_ref[0].astype(F32)
    pad_ref[HALO + tc:] = jnp.where(n < ntile - 1, hn_ref[0].astype(F32), 0.0)
    half = CONV_WIDTH // 2
    rows = CONV_ROWS
    span = sh_ref.shape[1]
    for c in range(1, SUBLANES):
        sh_ref[c - 1] = pad_ref[pl.ds(c, span), :]

    @pl.when((pl.program_id(0) == 0) & (n == 0))
    def _():
        for k in range(CONV_WIDTH):
            wb_ref[k] = jnp.broadcast_to(cw_ref[k:k + 1, :], (SUBLANES, CONV_INNER))

    def body(i, carry):
        r0 = pl.multiple_of(i * rows, rows)
        for l0 in range(0, CONV_INNER, CONV_LANES):
            ls = slice(l0, l0 + CONV_LANES)
            accs = [jnp.zeros((SUBLANES, CONV_LANES), F32) + cb_ref[:, ls] for _ in range(rows // SUBLANES)]
            for k in range(CONV_WIDTH):
                o = HALO - half + k
                w = wb_ref[k, :, ls]
                for r in range(rows // SUBLANES):
                    a0 = r0 + o - o % SUBLANES + r * SUBLANES
                    if o % SUBLANES == 0:
                        src = pad_ref[pl.ds(a0, SUBLANES), ls]
                    else:
                        src = sh_ref[o % SUBLANES - 1, pl.ds(a0, SUBLANES), ls]
                    accs[r] = accs[r] + src * w
            for r in range(rows // SUBLANES):
                cv_ref[pl.ds(r0 + r * SUBLANES, SUBLANES), ls] = accs[r]
        return carry

    lax.fori_loop(0, tc // rows, body, 0)
    cv = cv_ref[...]
    mu = jnp.mean(cv, axis=-1, keepdims=True)
    cen = cv - mu
    var = jnp.mean(cen * cen, axis=-1, keepdims=True)
    hn = _silu(cen * lax.rsqrt(var + NORM_EPS) * lng_ref[...] + lnb_ref[...])
    hm = (hn * sg_ref[0].astype(F32)).astype(BF16)
    y = _dot(hm, wo_ref[...])
    out_ref[0] = _tail(y, x_ref[0], p_ref[0], gpost_ref[...], wg_ref[...], wp_ref[...], gple_ref[...])


def _odd_out(h3, sg3, x3, p4, layer, cw, cb, lng, lnb, wo, gpost, wg, wp, gple):
    b, l, _ = h3.shape
    tc = min(CONV_TILE, l)
    ntile = l // tc
    cur = lambda bi, n: (bi, n, 0)
    prev, nxt = _halo_maps(tc // HALO, l // HALO)
    consts = (cw, cb, lng, lnb, wo, gpost, wg, wp, gple)
    return pl.pallas_call(
        functools.partial(_odd_out_kernel, tc=tc, ntile=ntile),
        grid=(b, ntile),
        in_specs=[pl.BlockSpec((1, tc, CONV_INNER), cur), pl.BlockSpec((1, HALO, CONV_INNER), prev),
                  pl.BlockSpec((1, HALO, CONV_INNER), nxt), pl.BlockSpec((1, tc, CONV_INNER), cur),
                  pl.BlockSpec((1, tc, D_MODEL), cur),
                  pl.BlockSpec((None, 1, tc, PLE_DIM), lambda bi, n: (layer, bi, n, 0))]
                 + [_const_spec(c.shape) for c in consts],
        out_specs=pl.BlockSpec((1, tc, D_MODEL), cur),
        out_shape=jax.ShapeDtypeStruct((b, l, D_MODEL), F32),
        scratch_shapes=[pltpu.VMEM((tc + 2 * HALO, CONV_INNER), F32),
                        pltpu.VMEM((SUBLANES - 1, tc + 2 * HALO - SUBLANES, CONV_INNER), F32),
                        pltpu.VMEM((CONV_WIDTH, SUBLANES, CONV_INNER), F32),
                        pltpu.VMEM((tc, CONV_INNER), F32)],
        compiler_params=_params("arbitrary", "arbitrary"),
        name="odd_out",
    )(h3, h3, h3, sg3, x3, p4, *consts)


def _rope_tables(length):
    inv = 1.0 / (jnp.float32(ROPE_THETA) ** (jnp.arange(0, ROT_DIM, 2, dtype=F32) / ROT_DIM))
    ang = jnp.arange(length, dtype=F32)[:, None] * inv[None, :]
    cos, sin = jnp.cos(ang), jnp.sin(ang)
    m = jnp.arange(LANES) % HEAD_DIM
    idx = m % ROT_HALF
    cos_l = jnp.where(m < ROT_DIM, cos[:, idx], 1.0)
    sin_l = sin[:, idx]
    sa = jnp.where((m >= ROT_HALF) & (m < ROT_DIM), sin_l, 0.0)
    sb = jnp.where(m < ROT_HALF, -sin_l, 0.0)
    return cos_l, sa, sb, cos.T, sin.T


def _row(v):
    return v.reshape(1, -1).astype(F32)


def _pad_rows(w, rows):
    return jnp.concatenate([w, jnp.zeros((rows - w.shape[0],) + w.shape[1:], w.dtype)], axis=0)


def _pad_lanes(v, lanes=LANES):
    v = v.reshape(1, -1).astype(F32)
    return jnp.concatenate([v, jnp.zeros((1, lanes - v.shape[1]), F32)], axis=1)


def _even_weights(j, ev_w_in, ev_w_out, attn_sink, ssm_conv_w, ssm_conv_b, ssm_dt_bias, ssm_a_log, ssm_d,
                  ssm_norm):
    w = ev_w_in[j].astype(BF16)
    c0 = 0
    cols = []
    for width in (ATTN_Q, ATTN_KV, ATTN_KV, ATTN_Q, SSM_XBC, SSM_INNER, 2 * SSM_HEADS):
        cols.append(w[:, c0:c0 + width])
        c0 += width
    wq, wk, wv, wga, wxbc, wz, wdt = cols
    wdt = jnp.concatenate([wdt, jnp.zeros((D_MODEL, LANES - 2 * SSM_HEADS), BF16)], axis=1)
    wo = ev_w_out[j].astype(BF16)
    return dict(
        wqt=wq.T, wk=wk, wvt=wv.T, wga=wga, wxbc=wxbc, wz=wz, wdt=wdt, woa=wo[:ATTN_Q], wob=wo[ATTN_Q:],
        sink=attn_sink[j].astype(F32),
        conv_w=_pad_rows(ssm_conv_w[j].astype(F32), SUBLANES), conv_b=_row(ssm_conv_b[j]),
        dt_bias=_pad_lanes(ssm_dt_bias[j]), a_log=_pad_lanes(ssm_a_log[j]),
        dskip=_row(jnp.repeat(ssm_d[j].astype(F32), SSM_HEAD_DIM)), ssm_norm=_row(ssm_norm[j]))


def _odd_weights(j, od_w_in, od_conv_w, od_conv_b, od_ln_g, od_ln_b, od_w_out):
    w = od_w_in[j].astype(BF16)
    return dict(
        wa=w[:, :CONV_INNER], wb=w[:, CONV_INNER:2 * CONV_INNER], wg=w[:, 2 * CONV_INNER:],
        conv_w=_pad_rows(od_conv_w[j].astype(F32), 4 * SUBLANES), conv_b=_row(od_conv_b[j]),
        ln_g=_row(od_ln_g[j]), ln_b=_row(od_ln_b[j]), wo=od_w_out[j].astype(BF16))


def _trunk(x, p, layers, rope):
    b, l, _ = x.shape
    t = b * l
    flat = lambda a: a.reshape(t, a.shape[-1])
    seq = lambda a: a.reshape(b, l, a.shape[-1])
    p3 = p.reshape(p.shape[0], t, PLE_DIM)
    for i, lw in enumerate(layers):
        common = (lw["gpost"], lw["ple_wg"], lw["ple_wp"], lw["gple"])
        if i % 2 == 0:
            qt, k, vt, ga, xbcc, z, dt = _even_in(x, lw["gpre"], lw, rope)
            o_attn = _attention(lw["sink"], qt, k, vt, ga)
            y_f = _ssd(xbcc, dt, lw["dt_bias"], lw["a_log"], rev=False)
            o_ssm = _ssd(xbcc, dt, lw["dt_bias"], lw["a_log"], rev=True,
                         extra=(y_f, z, lw["dskip"], lw["ssm_norm"]))
            x = seq(_even_out(flat(o_attn), flat(o_ssm), flat(x), p3, i, lw["woa"], lw["wob"], *common, l))
        else:
            h, sg = _odd_in(flat(x), lw["gpre"], lw["wa"], lw["wb"], lw["wg"], l)
            x = _odd_out(seq(h), seq(sg), x, p, i, lw["conv_w"], lw["conv_b"], lw["ln_g"], lw["ln_b"], lw["wo"],
                         *common)
    return x


def kernel(x_prompt, x_sample, p_prompt, p_sample, norm_pre, norm_post, ple_w_gate, ple_w_proj, ple_norm, ev_w_in, ev_w_out, attn_sink, ssm_conv_w, ssm_conv_b, ssm_dt_bias, ssm_a_log, ssm_d, ssm_norm, od_w_in, od_conv_w, od_conv_b, od_ln_g, od_ln_b, od_w_out):
    layers = []
    for i in range(DEPTH):
        j = i // 2
        if i % 2 == 0:
            lw = _even_weights(j, ev_w_in, ev_w_out, attn_sink, ssm_conv_w, ssm_conv_b, ssm_dt_bias, ssm_a_log,
                               ssm_d, ssm_norm)
        else:
            lw = _odd_weights(j, od_w_in, od_conv_w, od_conv_b, od_ln_g, od_ln_b, od_w_out)
        lw.update(gpre=_row(norm_pre[i]), gpost=_row(norm_post[i]), ple_wg=ple_w_gate[i].astype(BF16),
                  ple_wp=ple_w_proj[i].astype(BF16), gple=_row(ple_norm[i]))
        layers.append(lw)
    y_prompt = _trunk(x_prompt, p_prompt, layers, _rope_tables(x_prompt.shape[1]))
    y_sample = _trunk(x_sample, p_sample, layers, _rope_tables(x_sample.shape[1]))
    return (y_prompt, y_sample)
```
